```python
import math
import jax, jax.numpy as jnp
from jax import lax
import numpy as np

D_MODEL = 1024
BATCH = 32
SEQ = 2048
DEPTH = 1

HEAD_DIM = 64
MIX_WIDTH = D_MODEL
N_HEADS_A = (MIX_WIDTH // 2) // HEAD_DIM
N_HEADS_B = (MIX_WIDTH // 2) // HEAD_DIM
D_NOPE = HEAD_DIM
D_ROPE = HEAD_DIM // 2
KV_RANK = 2 * HEAD_DIM
V_DIM = HEAD_DIM
IDX_HEADS = 8
IDX_DIM = HEAD_DIM
TOPK_MAX = 256
Q_BLOCK = 128
DILATED_PATTERNS = ((128, 1), (512, 4), (2048, 16))
CROSS_HEADS = 4
CROSS_HEAD_DIM = D_MODEL // CROSS_HEADS
MEM_TOKENS = 256
D_FF = 4 * D_MODEL
ROPE_THETA = 10000.0
NORM_EPS = 1e-6

IN_SIZES = (N_HEADS_A * D_NOPE,
            N_HEADS_A * D_ROPE,
            KV_RANK,
            D_ROPE,
            IDX_HEADS * IDX_DIM,
            IDX_DIM,
            IDX_HEADS,
            3 * N_HEADS_B * HEAD_DIM)
IN_COLS = 512 + 256 + 128 + 32 + 512 + 64 + 8 + 1536 if D_MODEL == 1024 else int(np.sum(IN_SIZES))

kernel_name = "hybrid_dsa_dilated_xattn_block"


def rmsnorm(x, g):
    xf = x.astype(jnp.float32)
    y = xf * lax.rsqrt(jnp.mean(xf * xf, axis=-1, keepdims=True) + NORM_EPS)
    return (y * g.astype(jnp.float32)).astype(x.dtype)


def rope(x):
    S, d = x.shape[1], x.shape[-1]
    inv = ROPE_THETA ** (-jnp.arange(0, d, 2, dtype=jnp.float32) / d)
    ang = jnp.arange(S, dtype=jnp.float32)[:, None] * inv[None, :]
    cos = jnp.cos(ang)[None, :, None, :].astype(x.dtype)
    sin = jnp.sin(ang)[None, :, None, :].astype(x.dtype)
    x1, x2 = jnp.split(x, 2, axis=-1)
    return jnp.concatenate([x1 * cos - x2 * sin, x2 * cos + x1 * sin], axis=-1)


def dsa_sparse_attention(q_nope, q_rope, c_kv, k_rope, q_idx, k_idx, w_idx, w_uk, w_uv):
    B, S = q_nope.shape[0], q_nope.shape[1]
    topk = min(TOPK_MAX, S // 4)
    nb = S // Q_BLOCK
    scale = (D_NOPE + D_ROPE) ** -0.5
    q_lat = jnp.einsum('bshn,hnc->bshc', q_nope, w_uk)
    w_eff = w_idx.astype(jnp.float32) * (IDX_HEADS ** -0.5) * (IDX_DIM ** -0.5)
    key_pos = jnp.arange(S)
    gather = jax.vmap(lambda table, ix: table[ix])

    def to_blocks(a):
        return jnp.moveaxis(a.reshape((B, nb, Q_BLOCK) + a.shape[2:]), 1, 0)

    def block(args):
        ql, qr, qi, wi, blk = args
        q_pos = blk * Q_BLOCK + jnp.arange(Q_BLOCK)
        causal = key_pos[None, :] <= q_pos[:, None]
        logits = jnp.einsum('bqhd,bsd->bhqs', qi, k_idx).astype(jnp.float32)
        index = jnp.einsum('bhqs,bqh->bqs', jax.nn.relu(logits), wi)
        index = jnp.where(causal[None], index, -jnp.inf)
        _, sel = lax.top_k(index, topk)
        valid = sel <= q_pos[None, :, None]
        c_sel = gather(c_kv, sel)
        r_sel = gather(k_rope, sel)
        s = (jnp.einsum('bqhc,bqkc->bhqk', ql, c_sel)
             + jnp.einsum('bqhr,bqkr->bhqk', qr, r_sel)).astype(jnp.float32) * scale
        s = jnp.where(valid[:, None], s, -jnp.inf)
        p = jax.nn.softmax(s, axis=-1).astype(c_sel.dtype)
        return jnp.einsum('bhqk,bqkc->bqhc', p, c_sel)

    o_lat = lax.map(block, (to_blocks(q_lat), to_blocks(q_rope), to_blocks(q_idx),
                            to_blocks(w_eff), jnp.arange(nb)))
    o_lat = jnp.moveaxis(o_lat, 0, 1).reshape(B, S, N_HEADS_A, KV_RANK)
    return jnp.einsum('bshc,hcv->bshv', o_lat, w_uv)


def dilated_pattern(q, k, v, window, dilation):
    B, H, S, d = q.shape
    n = S // dilation
    w_sub = window // dilation
    nb = -(-n // Q_BLOCK)
    pad = nb * Q_BLOCK - n

    def split(a):
        a = a.reshape(B, H, n, dilation, d).transpose(0, 1, 3, 2, 4)
        a = jnp.pad(a, ((0, 0), (0, 0), (0, 0), (0, pad), (0, 0)))
        return a.reshape(B, H, dilation, nb, Q_BLOCK, d)

    def with_prev(a):
        prev = jnp.concatenate([jnp.zeros_like(a[:, :, :, :1]), a[:, :, :, :-1]], axis=3)
        return jnp.concatenate([prev, a], axis=4)

    qb = split(q)
    kk = with_prev(split(k))
    vv = with_prev(split(v))
    s = jnp.einsum('bhrnqd,bhrnkd->bhrnqk', qb, kk).astype(jnp.float32) * (d ** -0.5)
    rel = (jnp.arange(Q_BLOCK)[:, None] + Q_BLOCK) - jnp.arange(2 * Q_BLOCK)[None, :]
    band = (rel >= 0) & (rel <= w_sub)
    has_prev = (jnp.arange(nb)[:, None, None] > 0) | (jnp.arange(2 * Q_BLOCK)[None, None, :] >= Q_BLOCK)
    mask = band[None] & has_prev
    s = jnp.where(mask, s, -jnp.inf)
    lse = jax.nn.logsumexp(s, axis=-1)
    p = jnp.exp(s - lse[..., None]).astype(v.dtype)
    o = jnp.einsum('bhrnqk,bhrnkd->bhrnqd', p, vv)

    def merge(a, tail):
        a = a.reshape((B, H, dilation, nb * Q_BLOCK) + tail)[:, :, :, :n]
        return jnp.swapaxes(a, 2, 3).reshape((B, H, S) + tail)

    return merge(o, (d,)), merge(lse, ())


def dilated_attention(q, k, v):
    outs, lses = [], []
    for window, dilation in DILATED_PATTERNS:
        o, l = dilated_pattern(q, k, v, window, dilation)
        outs.append(o)
        lses.append(l)
    alpha = jax.nn.softmax(jnp.stack(lses, axis=0), axis=0).astype(q.dtype)
    return jnp.sum(alpha[..., None] * jnp.stack(outs, axis=0), axis=0)


def setup_inputs(seed: int = 0) -> dict:
    key = jax.random.key(seed)
    ks = jax.random.split(key, 20)
    f32 = jnp.float32

    def w(k, shape, fan_in):
        return jax.random.normal(k, shape, f32) * (fan_in ** -0.5)

    def gain(k, shape):
        return 1.0 + 0.02 * jax.random.normal(k, shape, f32)

    L = DEPTH
    return {
        "x": jax.random.normal(ks[0], (BATCH, SEQ, D_MODEL), f32),
        "mem": jax.random.normal(ks[1], (BATCH, MEM_TOKENS, D_MODEL), f32),
        "norm_mix_g": gain(ks[2], (L, D_MODEL)),
        "w_in": w(ks[3], (L, D_MODEL, IN_COLS), D_MODEL),
        "kv_norm_g": gain(ks[4], (L, KV_RANK)),
        "w_uk": w(ks[5], (L, N_HEADS_A, D_NOPE, KV_RANK), D_NOPE),
        "w_uv": w(ks[6], (L, N_HEADS_A, KV_RANK, V_DIM), KV_RANK),
        "w_out": w(ks[7], (L, N_HEADS_A * V_DIM + N_HEADS_B * HEAD_DIM, D_MODEL), MIX_WIDTH),
        "norm_cross_g": gain(ks[8], (L, D_MODEL)),
        "norm_mem_g": gain(ks[9], (L, D_MODEL)),
        "w_q_cross": w(ks[10], (L, D_MODEL, CROSS_HEADS * CROSS_HEAD_DIM), D_MODEL),
        "w_kv_cross": w(ks[11], (L, D_MODEL, 2 * CROSS_HEADS * CROSS_HEAD_DIM), D_MODEL),
        "w_o_cross": w(ks[12], (L, CROSS_HEADS * CROSS_HEAD_DIM, D_MODEL), CROSS_HEADS * CROSS_HEAD_DIM),
        "norm_mlp_g": gain(ks[13], (L, D_MODEL)),
        "w_up": w(ks[14], (L, D_MODEL, D_FF), D_MODEL),
        "w_down": w(ks[15], (L, D_FF, D_MODEL), D_FF),
        "norm_final_g": gain(ks[16], (D_MODEL,)),
    }


def reference(x, mem, norm_mix_g, w_in, kv_norm_g, w_uk, w_uv, w_out, norm_cross_g, norm_mem_g,
              w_q_cross, w_kv_cross, w_o_cross, norm_mlp_g, w_up, w_down, norm_final_g):
    B, S, _ = x.shape
    M = mem.shape[1]
    split_at = [int(c) for c in np.cumsum(IN_SIZES)[:-1]]
    for l in range(DEPTH):
        h = rmsnorm(x, norm_mix_g[l])
        proj = h @ w_in[l]
        p_qn, p_qr, p_ckv, p_kr, p_qi, p_ki, p_wi, p_qkv = jnp.split(proj, split_at, axis=-1)
        q_nope = p_qn.reshape(B, S, N_HEADS_A, D_NOPE)
        q_rope = rope(p_qr.reshape(B, S, N_HEADS_A, D_ROPE))
        c_kv = rmsnorm(p_ckv, kv_norm_g[l])
        k_rope = rope(p_kr[:, :, None, :])[:, :, 0]
        q_idx = rope(p_qi.reshape(B, S, IDX_HEADS, IDX_DIM))
        k_idx = rope(p_ki[:, :, None, :])[:, :, 0]
        o_a = dsa_sparse_attention(q_nope, q_rope, c_kv, k_rope, q_idx, k_idx, p_wi, w_uk[l], w_uv[l])
        qkv = p_qkv.reshape(B, S, 3, N_HEADS_B, HEAD_DIM)
        q_b = rope(qkv[:, :, 0]).transpose(0, 2, 1, 3)
        k_b = rope(qkv[:, :, 1]).transpose(0, 2, 1, 3)
        v_b = qkv[:, :, 2].transpose(0, 2, 1, 3)
        o_b = dilated_attention(q_b, k_b, v_b).transpose(0, 2, 1, 3)
        heads = jnp.concatenate([o_a.reshape(B, S, N_HEADS_A * V_DIM),
                                 o_b.reshape(B, S, N_HEADS_B * HEAD_DIM)], axis=-1)
        x = x + heads @ w_out[l]
        hc = rmsnorm(x, norm_cross_g[l])
        m = rmsnorm(mem, norm_mem_g[l])
        qc = (hc @ w_q_cross[l]).reshape(B, S, CROSS_HEADS, CROSS_HEAD_DIM)
        kvc = (m @ w_kv_cross[l]).reshape(B, M, 2, CROSS_HEADS, CROSS_HEAD_DIM)
        sc = jnp.einsum('bshd,bmhd->bhsm', qc, kvc[:, :, 0]).astype(jnp.float32) * (CROSS_HEAD_DIM ** -0.5)
        pc = jax.nn.softmax(sc, axis=-1).astype(x.dtype)
        oc = jnp.einsum('bhsm,bmhd->bshd', pc, kvc[:, :, 1]).reshape(B, S, CROSS_HEADS * CROSS_HEAD_DIM)
        x = x + oc @ w_o_cross[l]
        hm = rmsnorm(x, norm_mlp_g[l])
        x = x + jnp.square(jax.nn.relu(hm @ w_up[l])) @ w_down[l]
    return rmsnorm(x, norm_final_g)
```

```python
import functools

import numpy as np
import jax
import jax.numpy as jnp
from jax import lax
from jax.experimental import pallas as pl
from jax.experimental.pallas import tpu as pltpu

F32 = jnp.float32
BF16 = jnp.bfloat16
I32 = jnp.int32

D_MODEL = 1024
HEAD_DIM = 64
N_HEADS_A = 8
N_HEADS_B = 8
D_NOPE = 64
D_ROPE = 32
KV_RANK = 128
V_DIM = 64
IDX_HEADS = 8
IDX_DIM = 64
TOPK_MAX = 256
Q_BLOCK = 128
CROSS_HEADS = 4
CROSS_HEAD_DIM = 256
MEM_TOKENS = 256
D_FF = 4096
ROPE_THETA = 10000.0
NORM_EPS = 1e-6

LANES = 128
VMEM_LIMIT = 48 * 1024 * 1024
NEG = -1e30
INT_MIN = -2 ** 31

_OFF = np.cumsum([0, 512, 256, 128, 32, 512, 64, 8, 1536])
C_QN, C_QR, C_CKV, C_MISC, C_QI, C_QB, C_KB, C_VB, C_END = 0, 512, 768, 896, 1024, 1536, 2048, 2560, 3072
MISC_KR, MISC_KI, MISC_WI = 0, 32, 96


def _dot(a, b):
    return jnp.dot(a, b, preferred_element_type=F32)


def _dot_nt(a, b):
    return lax.dot_general(a, b, (((1,), (1,)), ((), ())), preferred_element_type=F32)


def _rms(x, g):
    return x * lax.rsqrt(jnp.mean(x * x, axis=-1, keepdims=True) + NORM_EPS) * g


def _rope_tables(seq):
    pos = jnp.arange(seq, dtype=F32)[:, None]
    lane = np.arange(LANES)

    def tables(d, lanes_local, active):
        half = d // 2
        inv = ROPE_THETA ** (-jnp.arange(0, d, 2, dtype=F32) / d)
        ang = pos * inv[None, :]
        cos, sin = jnp.cos(ang), jnp.sin(ang)
        f = (lanes_local % d) % half
        first = jnp.asarray(((lanes_local % d) < half) & active)[None, :]
        second = jnp.asarray(((lanes_local % d) >= half) & active)[None, :]
        act = jnp.asarray(active)[None, :]
        c = jnp.where(act, cos[:, f], 0.0)
        sa = jnp.where(first, -sin[:, f], 0.0)
        sb = jnp.where(second, sin[:, f], 0.0)
        return c, sa, sb

    all_on = np.ones(LANES, bool)
    c64, sa64, sb64 = tables(64, lane, all_on)
    c32, sa32, sb32 = tables(32, lane, all_on)
    kr_on = lane < MISC_KI
    ki_on = (lane >= MISC_KI) & (lane < MISC_WI)
    ckr, sa16m, sb16m = tables(32, lane, kr_on)
    cki, sa32m, sb32m = tables(64, lane - MISC_KI, ki_on)
    w_scale = (IDX_HEADS ** -0.5) * (IDX_DIM ** -0.5)
    wi_on = jnp.asarray((lane >= MISC_WI) & (lane < MISC_WI + IDX_HEADS))[None, :]
    cosm = ckr + cki + jnp.where(wi_on, w_scale, 0.0)
    return jnp.stack([c64, sa64, sb64, c32, sa32, sb32, cosm, sa16m, sb16m, sa32m, sb32m], axis=0)


def _rope_lanes(x, cos, sa, sb, half):
    outs = []
    for c in range(x.shape[1] // LANES):
        xs = x[:, c * LANES:(c + 1) * LANES]
        outs.append(xs * cos + pltpu.roll(xs, LANES - half, 1) * sa + pltpu.roll(xs, half, 1) * sb)
    return outs[0] if len(outs) == 1 else jnp.concatenate(outs, axis=1)


def _inproj_kernel(x_ref, g_ref, w_ref, wuk_ref, kvg_ref, tab_ref,
                   qlat_ref, qrope_ref, ckv_ref, misc_ref, qidx_ref, qb_ref, kb_ref, vb_ref):
    h = _rms(x_ref[...], g_ref[...]).astype(BF16)

    def proj(c0, c1):
        return _dot(h, w_ref[:, c0:c1])

    c64, sa64, sb64 = tab_ref[0], tab_ref[1], tab_ref[2]
    c32, sa32, sb32 = tab_ref[3], tab_ref[4], tab_ref[5]

    qn = proj(C_QN, C_QR).astype(BF16)
    qlat_ref[...] = _dot(qn, wuk_ref[...]).astype(BF16)
    qrope_ref[...] = _rope_lanes(proj(C_QR, C_CKV), c32, sa32, sb32, D_ROPE // 2).astype(BF16)
    ckv_ref[...] = _rms(proj(C_CKV, C_MISC), kvg_ref[...]).astype(BF16)
    pm = proj(C_MISC, C_QI)
    misc_ref[...] = (pm * tab_ref[6]
                     + pltpu.roll(pm, LANES - 16, 1) * tab_ref[7] + pltpu.roll(pm, 16, 1) * tab_ref[8]
                     + pltpu.roll(pm, LANES - 32, 1) * tab_ref[9] + pltpu.roll(pm, 32, 1) * tab_ref[10])
    qidx_ref[...] = _rope_lanes(proj(C_QI, C_QB), c64, sa64, sb64, IDX_DIM // 2).astype(BF16)
    qb_ref[...] = _rope_lanes(proj(C_QB, C_KB), c64, sa64, sb64, HEAD_DIM // 2)
    kb_ref[...] = _rope_lanes(proj(C_KB, C_VB), c64, sa64, sb64, HEAD_DIM // 2)
    vb_ref[...] = proj(C_VB, C_END)


def _inproj(x2, g, w_cat, wuk_bd, kvg, tabs, seq, tm):
    n = x2.shape[0]
    per_seq = seq // tm
    row = lambda i: (i, 0)
    const = lambda i: (0, 0)
    outs = [(D_MODEL, BF16), (N_HEADS_A * D_ROPE, BF16), (KV_RANK, BF16), (LANES, F32),
            (IDX_HEADS * IDX_DIM, BF16), (512, F32), (512, F32), (512, F32)]
    return pl.pallas_call(
        _inproj_kernel,
        grid=(n // tm,),
        in_specs=[pl.BlockSpec((tm, D_MODEL), row),
                  pl.BlockSpec((1, D_MODEL), const),
                  pl.BlockSpec((D_MODEL, C_END), const),
                  pl.BlockSpec((N_HEADS_A * D_NOPE, N_HEADS_A * KV_RANK), const),
                  pl.BlockSpec((1, KV_RANK), const),
                  pl.BlockSpec((11, tm, LANES), lambda i: (0, i % per_seq, 0))],
        out_specs=[pl.BlockSpec((tm, w), row) for w, _ in outs],
        out_shape=[jax.ShapeDtypeStruct((n, w), dt) for w, dt in outs],
        compiler_params=pltpu.CompilerParams(dimension_semantics=("arbitrary",), vmem_limit_bytes=VMEM_LIMIT),
        name="inproj",
    )(x2, g, w_cat, wuk_bd, kvg, tabs)


DSA_CK = 512


def _dsa_kernel(qlat_ref, qrope_ref, qidx_ref, weff_ref, kidx_ref, kcatT_ref, ckv_ref, wuv_ref,
                out_ref, ikey_ref, bias_ref, qall_ref, xp_ref, m_ref, l_ref, acc_ref, *, seq, topk):
    blk = pl.program_id(1)
    q0 = blk * Q_BLOCK
    nkc = (q0 + Q_BLOCK + DSA_CK - 1) // DSA_CK
    ck = DSA_CK
    scale = (D_NOPE + D_ROPE) ** -0.5

    qi_all = qidx_ref[...].reshape(IDX_HEADS * Q_BLOCK, IDX_DIM)
    t_pos = q0 + lax.broadcasted_iota(I32, (ck, Q_BLOCK), 1)

    def index_chunk(c, carry):
        k0 = pl.multiple_of(c * ck, ck)
        lg = _dot_nt(kidx_ref[pl.ds(k0, ck), :], qi_all)
        idx = jnp.zeros((ck, Q_BLOCK), F32)
        for h in range(IDX_HEADS):
            idx = idx + jnp.maximum(lg[:, h * Q_BLOCK:(h + 1) * Q_BLOCK], 0.0) * weff_ref[h:h + 1, :]
        s_pos = k0 + lax.broadcasted_iota(I32, (ck, Q_BLOCK), 0)
        idx = jnp.where(idx == 0.0, 0.0, idx)
        idx = jnp.where(s_pos <= t_pos, idx, -jnp.inf)
        bits = pltpu.bitcast(idx, I32)
        ikey_ref[pl.ds(k0, ck), :] = bits ^ ((bits >> 31) & 0x7FFFFFFF)
        return carry

    lax.fori_loop(0, nkc, index_chunk, 0)

    def count(pred):
        def body(c, cnt):
            k0 = pl.multiple_of(c * ck, ck)
            v = ikey_ref[pl.ds(k0, ck), :]
            s_pos = k0 + lax.broadcasted_iota(I32, (ck, Q_BLOCK), 0)
            ind = pred(v, s_pos)
            return cnt + jnp.sum(ind.reshape(ck // 8, 8, Q_BLOCK), axis=0)
        cnt8 = lax.fori_loop(0, nkc, body, jnp.zeros((8, Q_BLOCK), I32))
        return jnp.sum(cnt8, axis=0, keepdims=True)

    thr = jnp.full((1, Q_BLOCK), INT_MIN, I32)
    for bit in range(31, -1, -1):
        cand = jnp.zeros((1, Q_BLOCK), I32) if bit == 31 else thr + np.int32(1 << bit)
        cnt = count(lambda v, p, cand=cand: jnp.where(v >= cand, 1, 0))
        thr = jnp.where(cnt >= topk, cand, thr)

    need = topk - count(lambda v, p: jnp.where(v > thr, 1, 0))
    n_ge = count(lambda v, p: jnp.where(v >= thr, 1, 0))
    xp_ref[...] = jnp.full((8, Q_BLOCK), seq, I32)

    @pl.when(jnp.max(n_ge) > topk)
    def _():
        x = jnp.zeros((1, Q_BLOCK), I32)
        for bit in range(int(np.log2(seq)) - 1, -1, -1):
            cand = x + np.int32(1 << bit)
            hc = count(lambda v, p, cand=cand: jnp.where(v == thr, jnp.where(p < cand, 1, 0), 0))
            x = jnp.where(hc < need, cand, x)
        xp_ref[...] = jnp.broadcast_to(x + 1, (8, Q_BLOCK))

    xp = xp_ref[0:1, :]

    def bias_chunk(c, carry):
        k0 = pl.multiple_of(c * ck, ck)
        v = ikey_ref[pl.ds(k0, ck), :]
        s_pos = k0 + lax.broadcasted_iota(I32, (ck, Q_BLOCK), 0)
        tie = jnp.where(s_pos < xp, 0.0, NEG)
        b = jnp.where(v > thr, 0.0, jnp.where(v == thr, tie, NEG))
        b = jnp.where(s_pos <= t_pos, b, NEG).astype(F32)
        bias_ref[:, pl.ds(k0, ck)] = b.T
        return carry

    lax.fori_loop(0, nkc, bias_chunk, 0)

    zpad = jnp.zeros((Q_BLOCK, 2 * KV_RANK - KV_RANK - D_ROPE), BF16)
    for h in range(N_HEADS_A):
        qall_ref[h * Q_BLOCK:(h + 1) * Q_BLOCK, :] = jnp.concatenate(
            [qlat_ref[:, h * KV_RANK:(h + 1) * KV_RANK], qrope_ref[:, h * D_ROPE:(h + 1) * D_ROPE], zpad], axis=1)
    rows = N_HEADS_A * Q_BLOCK
    m_ref[...] = jnp.full((rows, 1), NEG, F32)
    l_ref[...] = jnp.zeros((rows, 1), F32)
    acc_ref[...] = jnp.zeros((rows, KV_RANK), F32)

    def attn_chunk(c, carry):
        k0 = pl.multiple_of(c * ck, ck)
        s = _dot(qall_ref[...], kcatT_ref[:, pl.ds(k0, ck)]) * scale
        s = (s.reshape(N_HEADS_A, Q_BLOCK, ck) + bias_ref[:, pl.ds(k0, ck)][None]).reshape(rows, ck)
        m_old = m_ref[...]
        m_new = jnp.maximum(m_old, jnp.max(s, axis=1, keepdims=True))
        alpha = jnp.exp(m_old - m_new)
        p = jnp.exp(s - m_new)
        l_ref[...] = alpha * l_ref[...] + jnp.sum(p, axis=1, keepdims=True)
        acc_ref[...] = alpha * acc_ref[...] + _dot(p.astype(BF16), ckv_ref[pl.ds(k0, ck), :])
        m_ref[...] = m_new
        return carry

    lax.fori_loop(0, nkc, attn_chunk, 0)

    o = (acc_ref[...] / l_ref[...]).astype(BF16)
    o_lat = jnp.concatenate([o[h * Q_BLOCK:(h + 1) * Q_BLOCK, :] for h in range(N_HEADS_A)], axis=1)
    out_ref[...] = _dot(o_lat, wuv_ref[...]).astype(BF16)


def _dsa(qlat, qrope, qidx_h, weffT, kidx, kcatT, ckv, wuv_bd, seq):
    b = qlat.shape[0]
    nb = seq // Q_BLOCK
    topk = min(TOPK_MAX, seq // 4)
    blkmap = lambda i, j: (i, j, 0)
    seqmap = lambda i, j: (i, 0, 0)
    rows = N_HEADS_A * Q_BLOCK
    return pl.pallas_call(
        functools.partial(_dsa_kernel, seq=seq, topk=topk),
        grid=(b, nb),
        in_specs=[pl.BlockSpec((None, Q_BLOCK, N_HEADS_A * KV_RANK), blkmap),
                  pl.BlockSpec((None, Q_BLOCK, N_HEADS_A * D_ROPE), blkmap),
                  pl.BlockSpec((None, IDX_HEADS, Q_BLOCK, IDX_DIM), lambda i, j: (i, 0, j, 0)),
                  pl.BlockSpec((None, IDX_HEADS, Q_BLOCK), lambda i, j: (i, 0, j)),
                  pl.BlockSpec((None, seq, IDX_DIM), seqmap),
                  pl.BlockSpec((None, 2 * KV_RANK, seq), seqmap),
                  pl.BlockSpec((None, seq, KV_RANK), seqmap),
                  pl.BlockSpec((N_HEADS_A * KV_RANK, N_HEADS_A * V_DIM), lambda i, j: (0, 0))],
        out_specs=pl.BlockSpec((None, Q_BLOCK, N_HEADS_A * V_DIM), blkmap),
        out_shape=jax.ShapeDtypeStruct((b, seq, N_HEADS_A * V_DIM), BF16),
        scratch_shapes=[pltpu.VMEM((seq, Q_BLOCK), I32),
                        pltpu.VMEM((Q_BLOCK, seq), F32),
                        pltpu.VMEM((rows, 2 * KV_RANK), BF16),
                        pltpu.VMEM((8, Q_BLOCK), I32),
                        pltpu.VMEM((rows, 1), F32),
                        pltpu.VMEM((rows, 1), F32),
                        pltpu.VMEM((rows, KV_RANK), F32)],
        compiler_params=pltpu.CompilerParams(dimension_semantics=("arbitrary", "arbitrary"),
                                             vmem_limit_bytes=VMEM_LIMIT),
        name="dsa",
    )(qlat, qrope, qidx_h, weffT, kidx, kcatT, ckv, wuv_bd)


DILATED_PATTERNS = ((128, 1), (512, 4), (2048, 16))


def _dilated_kernel(q_ref, k_ref, v_ref, out_ref, acc_ref, m_ref, l_ref, *, seq):
    qb = Q_BLOCK
    lane = lax.broadcasted_iota(I32, (1, LANES), 1)
    head0 = lane < HEAD_DIM
    a_idx = lax.broadcasted_iota(I32, (qb, 2 * qb), 0)
    c_idx = lax.broadcasted_iota(I32, (qb, 2 * qb), 1)
    scale = HEAD_DIM ** -0.5

    def rows(ref, start, stride):
        if stride == 1:
            return ref[pl.ds(start, qb), :]
        return ref[pl.ds(start, qb, stride=stride), :]

    def put(ref, start, stride, val):
        if stride == 1:
            ref[pl.ds(start, qb), :] = val
        else:
            ref[pl.ds(start, qb, stride=stride), :] = val

    def block(q_start, kprev_start, kcur_start, stride, w_sub, has_prev, first):
        q = (rows(q_ref, q_start, stride) * scale).astype(BF16)
        if kprev_start is None:
            kk = rows(k_ref, kcur_start, stride)
            vv = rows(v_ref, kcur_start, stride).astype(BF16)
            mask = (lax.broadcasted_iota(I32, (qb, qb), 1) <= lax.broadcasted_iota(I32, (qb, qb), 0))
        else:
            kk = jnp.concatenate([rows(k_ref, kprev_start, stride), rows(k_ref, kcur_start, stride)], axis=0)
            vv = jnp.concatenate([rows(v_ref, kprev_start, stride), rows(v_ref, kcur_start, stride)],
                                 axis=0).astype(BF16)
            lo = jnp.where(has_prev, a_idx + (qb - w_sub), jnp.maximum(a_idx + (qb - w_sub), qb))
            mask = (c_idx >= lo) & (c_idx <= a_idx + qb)
        pvs, ms, ls = [], [], []
        for hh in range(2):
            khead = jnp.where(head0 if hh == 0 else jnp.logical_not(head0), kk, 0.0).astype(BF16)
            s = jnp.where(mask, _dot_nt(q, khead), NEG)
            m = jnp.max(s, axis=1, keepdims=True)
            p = jnp.exp(s - m)
            ls.append(jnp.sum(p, axis=1, keepdims=True))
            ms.append(m)
            pvs.append(_dot(p.astype(BF16), vv))
        pv = jnp.where(head0, pvs[0], pvs[1])
        mb = jnp.where(head0, ms[0], ms[1])
        lb = jnp.where(head0, ls[0], ls[1])
        if first:
            put(acc_ref, q_start, stride, pv)
            put(m_ref, q_start, stride, mb)
            put(l_ref, q_start, stride, lb)
        else:
            mo = rows(m_ref, q_start, stride)
            mn = jnp.maximum(mo, mb)
            a_old = jnp.exp(mo - mn)
            a_new = jnp.exp(mb - mn)
            put(acc_ref, q_start, stride, rows(acc_ref, q_start, stride) * a_old + pv * a_new)
            put(l_ref, q_start, stride, rows(l_ref, q_start, stride) * a_old + lb * a_new)
            put(m_ref, q_start, stride, mn)

    first = True
    for window, d in DILATED_PATTERNS:
        n = seq // d
        w_sub = window // d
        nb = n // qb
        if nb == 1:
            def body(r, carry, d=d, first=first):
                block(r, None, r, d, w_sub, None, first)
                return carry
            lax.fori_loop(0, d, body, 0)
        else:
            def body(j, carry, d=d, nb=nb, w_sub=w_sub, first=first):
                r = j // nb
                i = j % nb
                q_start = r + i * (qb * d)
                kprev = r + jnp.maximum(i - 1, 0) * (qb * d)
                block(q_start, kprev, q_start, d, w_sub, i > 0, first)
                return carry
            lax.fori_loop(0, d * nb, body, 0)
        first = False

    out_ref[...] = (acc_ref[...] / l_ref[...]).astype(BF16)


def _dilated(qb, kb, vb, seq):
    b = qb.shape[0]
    npair = N_HEADS_B * HEAD_DIM // LANES
    spec = pl.BlockSpec((None, seq, LANES), lambda i, j: (i, 0, j))
    return pl.pallas_call(
        functools.partial(_dilated_kernel, seq=seq),
        grid=(b, npair),
        in_specs=[spec, spec, spec],
        out_specs=spec,
        out_shape=jax.ShapeDtypeStruct((b, seq, N_HEADS_B * HEAD_DIM), BF16),
        scratch_shapes=[pltpu.VMEM((seq, LANES), F32)] * 3,
        compiler_params=pltpu.CompilerParams(dimension_semantics=("arbitrary", "arbitrary"),
                                             vmem_limit_bytes=VMEM_LIMIT),
        name="dilated",
    )(qb, kb, vb)


def _memkv_kernel(mem_ref, g_ref, w_ref, k_ref, v_ref):
    m = _rms(mem_ref[...], g_ref[...]).astype(BF16)
    kv = _dot(m, w_ref[...])
    k_ref[...] = kv[:, :D_MODEL].astype(BF16)
    v_ref[...] = kv[:, D_MODEL:].astype(BF16)


def _memkv(mem2, g, w_kv):
    n = mem2.shape[0]
    tm = MEM_TOKENS
    row = lambda i: (i, 0)
    const = lambda i: (0, 0)
    return pl.pallas_call(
        _memkv_kernel,
        grid=(n // tm,),
        in_specs=[pl.BlockSpec((tm, D_MODEL), row), pl.BlockSpec((1, D_MODEL), const),
                  pl.BlockSpec((D_MODEL, 2 * D_MODEL), const)],
        out_specs=[pl.BlockSpec((tm, D_MODEL), row)] * 2,
        out_shape=[jax.ShapeDtypeStruct((n, D_MODEL), BF16)] * 2,
        compiler_params=pltpu.CompilerParams(dimension_semantics=("arbitrary",), vmem_limit_bytes=VMEM_LIMIT),
        name="memkv",
    )(mem2, g, w_kv)


def _cross_kernel(x_ref, oa_ref, ob_ref, wout_ref, g_ref, wq_ref, kc_ref, vc_ref, wo_ref, out_ref):
    half = N_HEADS_A * V_DIM
    x1 = x_ref[...] + _dot(oa_ref[...], wout_ref[:half, :]) + _dot(ob_ref[...], wout_ref[half:, :])
    qc = _dot(_rms(x1, g_ref[...]).astype(BF16), wq_ref[...]).astype(BF16)
    scale = CROSS_HEAD_DIM ** -0.5
    ocs = []
    for h in range(CROSS_HEADS):
        sl = slice(h * CROSS_HEAD_DIM, (h + 1) * CROSS_HEAD_DIM)
        s = _dot_nt(qc[:, sl], kc_ref[:, sl]) * scale
        p = jnp.exp(s - jnp.max(s, axis=1, keepdims=True))
        p = p / jnp.sum(p, axis=1, keepdims=True)
        ocs.append(_dot(p.astype(BF16), vc_ref[:, sl]).astype(BF16))
    oc = jnp.concatenate(ocs, axis=1)
    out_ref[...] = x1 + _dot(oc, wo_ref[...])


def _cross(x2, oa, ob, w_out, g, w_q, kc, vc, w_o, seq, tm):
    n = x2.shape[0]
    per_seq = seq // tm
    row = lambda i: (i, 0)
    const = lambda i: (0, 0)
    memmap = lambda i: (i // per_seq, 0)
    half = N_HEADS_A * V_DIM
    return pl.pallas_call(
        _cross_kernel,
        grid=(n // tm,),
        in_specs=[pl.BlockSpec((tm, D_MODEL), row), pl.BlockSpec((tm, half), row), pl.BlockSpec((tm, half), row),
                  pl.BlockSpec((D_MODEL, D_MODEL), const), pl.BlockSpec((1, D_MODEL), const),
                  pl.BlockSpec((D_MODEL, D_MODEL), const),
                  pl.BlockSpec((MEM_TOKENS, D_MODEL), memmap), pl.BlockSpec((MEM_TOKENS, D_MODEL), memmap),
                  pl.BlockSpec((D_MODEL, D_MODEL), const)],
        out_specs=pl.BlockSpec((tm, D_MODEL), row),
        out_shape=jax.ShapeDtypeStruct((n, D_MODEL), F32),
        compiler_params=pltpu.CompilerParams(dimension_semantics=("arbitrary",), vmem_limit_bytes=VMEM_LIMIT),
        name="cross",
    )(x2, oa, ob, w_out, g, w_q, kc, vc, w_o)


MLP_FF_CHUNK = 1024


def _mlp_kernel(x_ref, g_ref, wup_ref, wdown_ref, gf_ref, out_ref):
    x = x_ref[...]
    hm = _rms(x, g_ref[...]).astype(BF16)
    y = x
    for c in range(D_FF // MLP_FF_CHUNK):
        sl = slice(c * MLP_FF_CHUNK, (c + 1) * MLP_FF_CHUNK)
        u = jnp.maximum(_dot(hm, wup_ref[:, sl]), 0.0)
        y = y + _dot((u * u).astype(BF16), wdown_ref[sl, :])
    out_ref[...] = _rms(y, gf_ref[...])


def _mlp(x2, g, w_up, w_down, gf, tm):
    n = x2.shape[0]
    row = lambda i: (i, 0)
    const = lambda i: (0, 0)
    return pl.pallas_call(
        _mlp_kernel,
        grid=(n // tm,),
        in_specs=[pl.BlockSpec((tm, D_MODEL), row), pl.BlockSpec((1, D_MODEL), const),
                  pl.BlockSpec((D_MODEL, D_FF), const), pl.BlockSpec((D_FF, D_MODEL), const),
                  pl.BlockSpec((1, D_MODEL), const)],
        out_specs=pl.BlockSpec((tm, D_MODEL), row),
        out_shape=jax.ShapeDtypeStruct((n, D_MODEL), F32),
        compiler_params=pltpu.CompilerParams(dimension_semantics=("arbitrary",), vmem_limit_bytes=VMEM_LIMIT),
        name="mlp",
    )(x2, g, w_up, w_down, gf)


def _block_diag(w):
    h, a, b = w.shape
    eye = jnp.eye(h, dtype=w.dtype)
    return (eye[:, None, :, None] * w[:, :, None, :]).reshape(h * a, h * b)


def kernel(x, mem, norm_mix_g, w_in, kv_norm_g, w_uk, w_uv, w_out, norm_cross_g, norm_mem_g,
           w_q_cross, w_kv_cross, w_o_cross, norm_mlp_g, w_up, w_down, norm_final_g):
    b, seq, _ = x.shape
    assert seq == 2048 and w_in.shape[0] == 1, "kernel is specialised to SEQ=2048, DEPTH=1"
    tm = 512

    wi = w_in[0]
    col = lambda k: wi[:, _OFF[k]:_OFF[k + 1]]
    misc = jnp.concatenate([col(3), col(5), col(6), jnp.zeros((D_MODEL, LANES - 104), F32)], axis=1)
    w_cat = jnp.concatenate([col(0), col(1), col(2), misc, col(4), col(7)], axis=1).astype(BF16)
    wuk_bd = _block_diag(w_uk[0]).astype(BF16)
    wuv_bd = _block_diag(w_uv[0]).astype(BF16)
    tabs = _rope_tables(seq)

    x2 = x.reshape(b * seq, D_MODEL)
    qlat, qrope, ckv, miscp, qidx, qb, kb, vb = _inproj(
        x2, norm_mix_g[0][None], w_cat, wuk_bd, kv_norm_g[0][None], tabs, seq, tm)

    miscp = miscp.reshape(b, seq, LANES)
    krope = miscp[:, :, MISC_KR:MISC_KR + D_ROPE].astype(BF16)
    kidx = miscp[:, :, MISC_KI:MISC_KI + IDX_DIM].astype(BF16)
    weffT = jnp.swapaxes(miscp[:, :, MISC_WI:MISC_WI + IDX_HEADS], 1, 2)
    ckv = ckv.reshape(b, seq, KV_RANK)
    kcat = jnp.concatenate([ckv, krope, jnp.zeros((b, seq, 2 * KV_RANK - KV_RANK - D_ROPE), BF16)], axis=2)
    kcatT = jnp.swapaxes(kcat, 1, 2)
    qidx_h = jnp.swapaxes(qidx.reshape(b, seq, IDX_HEADS, IDX_DIM), 1, 2)

    o_a = _dsa(qlat.reshape(b, seq, -1), qrope.reshape(b, seq, -1), qidx_h, weffT, kidx, kcatT, ckv, wuv_bd, seq)
    o_b = _dilated(qb.reshape(b, seq, -1), kb.reshape(b, seq, -1), vb.reshape(b, seq, -1), seq)

    kc, vc = _memkv(mem.reshape(b * MEM_TOKENS, D_MODEL), norm_mem_g[0][None], w_kv_cross[0].astype(BF16))
    xc = _cross(x2, o_a.reshape(b * seq, -1), o_b.reshape(b * seq, -1), w_out[0].astype(BF16),
                norm_cross_g[0][None], w_q_cross[0].astype(BF16), kc, vc, w_o_cross[0].astype(BF16), seq, tm)
    out = _mlp(xc, norm_mlp_g[0][None], w_up[0].astype(BF16), w_down[0].astype(BF16), norm_final_g[None], tm)
    return out.reshape(b, seq, D_MODEL)
```

```python
import functools

import numpy as np
import jax
import jax.numpy as jnp
from jax import lax
from jax.experimental import pallas as pl
from jax.experimental.pallas import tpu as pltpu

F32 = jnp.float32
BF16 = jnp.bfloat16
I32 = jnp.int32

D_MODEL = 1024
HEAD_DIM = 64
N_HEADS_A = 8
N_HEADS_B = 8
D_NOPE = 64
D_ROPE = 32
KV_RANK = 128
V_DIM = 64
IDX_HEADS = 8
IDX_DIM = 64
TOPK_MAX = 256
Q_BLOCK = 128
CROSS_HEADS = 4
CROSS_HEAD_DIM = 256
MEM_TOKENS = 256
D_FF = 4096
ROPE_THETA = 10000.0
NORM_EPS = 1e-6

LANES = 128
VMEM_LIMIT = 48 * 1024 * 1024
NEG = -1e30
INT_MIN = -2 ** 31

_OFF = np.cumsum([0, 512, 256, 128, 32, 512, 64, 8, 1536])
C_QN, C_QR, C_CKV, C_MISC, C_QI, C_QB, C_KB, C_VB, C_END = 0, 512, 768, 896, 1024, 1536, 2048, 2560, 3072
MISC_KR, MISC_KI, MISC_WI = 0, 32, 96


def _dot(a, b):
    return jnp.dot(a, b, preferred_element_type=F32)


def _dot_nt(a, b):
    return lax.dot_general(a, b, (((1,), (1,)), ((), ())), preferred_element_type=F32)


def _rms(x, g):
    return x * lax.rsqrt(jnp.mean(x * x, axis=-1, keepdims=True) + NORM_EPS) * g


def _rope_tables(seq):
    pos = jnp.arange(seq, dtype=F32)[:, None]
    lane = np.arange(LANES)

    def tables(d, lanes_local, active):
        half = d // 2
        inv = ROPE_THETA ** (-jnp.arange(0, d, 2, dtype=F32) / d)
        ang = pos * inv[None, :]
        cos, sin = jnp.cos(ang), jnp.sin(ang)
        f = (lanes_local % d) % half
        first = jnp.asarray(((lanes_local % d) < half) & active)[None, :]
        second = jnp.asarray(((lanes_local % d) >= half) & active)[None, :]
        act = jnp.asarray(active)[None, :]
        c = jnp.where(act, cos[:, f], 0.0)
        sa = jnp.where(first, -sin[:, f], 0.0)
        sb = jnp.where(second, sin[:, f], 0.0)
        return c, sa, sb

    all_on = np.ones(LANES, bool)
    c64, sa64, sb64 = tables(64, lane, all_on)
    c32, sa32, sb32 = tables(32, lane, all_on)
    kr_on = lane < MISC_KI
    ki_on = (lane >= MISC_KI) & (lane < MISC_WI)
    ckr, sa16m, sb16m = tables(32, lane, kr_on)
    cki, sa32m, sb32m = tables(64, lane - MISC_KI, ki_on)
    w_scale = (IDX_HEADS ** -0.5) * (IDX_DIM ** -0.5)
    wi_on = jnp.asarray((lane >= MISC_WI) & (lane < MISC_WI + IDX_HEADS))[None, :]
    cosm = ckr + cki + jnp.where(wi_on, w_scale, 0.0)
    return jnp.stack([c64, sa64, sb64, c32, sa32, sb32, cosm, sa16m, sb16m, sa32m, sb32m], axis=0)


def _rope_lanes(x, cos, sa, sb, half):
    outs = []
    for c in range(x.shape[1] // LANES):
        xs = x[:, c * LANES:(c + 1) * LANES]
        outs.append(xs * cos + pltpu.roll(xs, LANES - half, 1) * sa + pltpu.roll(xs, half, 1) * sb)
    return outs[0] if len(outs) == 1 else jnp.concatenate(outs, axis=1)


def _inproj_kernel(x_ref, g_ref, w_ref, wuk_ref, kvg_ref, tab_ref,
                   qlat_ref, qrope_ref, ckv_ref, misc_ref, qidx_ref, qb_ref, kb_ref, vb_ref):
    h = _rms(x_ref[...], g_ref[...]).astype(BF16)

    def proj(c0, c1):
        return _dot(h, w_ref[:, c0:c1])

    c64, sa64, sb64 = tab_ref[0], tab_ref[1], tab_ref[2]
    c32, sa32, sb32 = tab_ref[3], tab_ref[4], tab_ref[5]

    qn = proj(C_QN, C_QR).astype(BF16)
    qlat_ref[...] = _dot(qn, wuk_ref[...]).astype(BF16)
    qrope_ref[...] = _rope_lanes(proj(C_QR, C_CKV), c32, sa32, sb32, D_ROPE // 2).astype(BF16)
    ckv_ref[...] = _rms(proj(C_CKV, C_MISC), kvg_ref[...]).astype(BF16)
    pm = proj(C_MISC, C_QI)
    misc_ref[...] = (pm * tab_ref[6]
                     + pltpu.roll(pm, LANES - 16, 1) * tab_ref[7] + pltpu.roll(pm, 16, 1) * tab_ref[8]
                     + pltpu.roll(pm, LANES - 32, 1) * tab_ref[9] + pltpu.roll(pm, 32, 1) * tab_ref[10])
    qidx_ref[...] = _rope_lanes(proj(C_QI, C_QB), c64, sa64, sb64, IDX_DIM // 2).astype(BF16)
    qb_ref[...] = _rope_lanes(proj(C_QB, C_KB), c64, sa64, sb64, HEAD_DIM // 2)
    kb_ref[...] = _rope_lanes(proj(C_KB, C_VB), c64, sa64, sb64, HEAD_DIM // 2)
    vb_ref[...] = proj(C_VB, C_END)


def _inproj(x2, g, w_cat, wuk_bd, kvg, tabs, seq, tm):
    n = x2.shape[0]
    per_seq = seq // tm
    row = lambda i: (i, 0)
    const = lambda i: (0, 0)
    outs = [(D_MODEL, BF16), (N_HEADS_A * D_ROPE, BF16), (KV_RANK, BF16), (LANES, F32),
            (IDX_HEADS * IDX_DIM, BF16), (512, F32), (512, F32), (512, F32)]
    return pl.pallas_call(
        _inproj_kernel,
        grid=(n // tm,),
        in_specs=[pl.BlockSpec((tm, D_MODEL), row),
                  pl.BlockSpec((1, D_MODEL), const),
                  pl.BlockSpec((D_MODEL, C_END), const),
                  pl.BlockSpec((N_HEADS_A * D_NOPE, N_HEADS_A * KV_RANK), const),
                  pl.BlockSpec((1, KV_RANK), const),
                  pl.BlockSpec((11, tm, LANES), lambda i: (0, i % per_seq, 0))],
        out_specs=[pl.BlockSpec((tm, w), row) for w, _ in outs],
        out_shape=[jax.ShapeDtypeStruct((n, w), dt) for w, dt in outs],
        compiler_params=pltpu.CompilerParams(dimension_semantics=("arbitrary",), vmem_limit_bytes=VMEM_LIMIT),
        name="inproj",
    )(x2, g, w_cat, wuk_bd, kvg, tabs)


DSA_CK = 512
DSA_HEAD_GROUP = 2


def _dsa_kernel(qlat_ref, qrope_ref, qidx_ref, weff_ref, kidx_ref, kcatT_ref, ckv_ref, wuv_ref,
                out_ref, ikey_ref, bias_ref, qall_ref, xp_ref, m_ref, l_ref, acc_ref, *, seq, topk):
    blk = pl.program_id(1)
    q0 = blk * Q_BLOCK
    nkc = (q0 + Q_BLOCK + DSA_CK - 1) // DSA_CK
    ck = DSA_CK
    scale = (D_NOPE + D_ROPE) ** -0.5

    qi_all = qidx_ref[...].reshape(IDX_HEADS * Q_BLOCK, IDX_DIM)
    t_pos = q0 + lax.broadcasted_iota(I32, (ck, Q_BLOCK), 1)

    def index_chunk(c, carry):
        k0 = pl.multiple_of(c * ck, ck)
        kc = kidx_ref[pl.ds(k0, ck), :]
        idx = jnp.zeros((ck, Q_BLOCK), F32)
        for g in range(IDX_HEADS // 2):
            lg = _dot_nt(kc, qi_all[2 * g * Q_BLOCK:(2 * g + 2) * Q_BLOCK, :])
            for hh in range(2):
                h = 2 * g + hh
                idx = idx + jnp.maximum(lg[:, hh * Q_BLOCK:(hh + 1) * Q_BLOCK], 0.0) * weff_ref[h:h + 1, :]
        s_pos = k0 + lax.broadcasted_iota(I32, (ck, Q_BLOCK), 0)
        idx = jnp.where(idx == 0.0, 0.0, idx)
        idx = jnp.where(s_pos <= t_pos, idx, -jnp.inf)
        bits = pltpu.bitcast(idx, I32)
        ikey_ref[pl.ds(k0, ck), :] = bits ^ ((bits >> 31) & 0x7FFFFFFF)
        return carry

    lax.fori_loop(0, nkc, index_chunk, 0)

    def count(pred):
        def body(c, cnt):
            k0 = pl.multiple_of(c * ck, ck)
            v = ikey_ref[pl.ds(k0, ck), :]
            s_pos = k0 + lax.broadcasted_iota(I32, (ck, Q_BLOCK), 0)
            ind = pred(v, s_pos)
            return cnt + jnp.sum(ind.reshape(ck // 8, 8, Q_BLOCK), axis=0)
        cnt8 = lax.fori_loop(0, nkc, body, jnp.zeros((8, Q_BLOCK), I32))
        return jnp.sum(cnt8, axis=0, keepdims=True)

    thr = jnp.full((1, Q_BLOCK), INT_MIN, I32)
    for bit in range(31, -1, -1):
        cand = jnp.zeros((1, Q_BLOCK), I32) if bit == 31 else thr + np.int32(1 << bit)
        cnt = count(lambda v, p, cand=cand: jnp.where(v >= cand, 1, 0))
        thr = jnp.where(cnt >= topk, cand, thr)

    need = topk - count(lambda v, p: jnp.where(v > thr, 1, 0))
    n_ge = count(lambda v, p: jnp.where(v >= thr, 1, 0))
    xp_ref[...] = jnp.full((8, Q_BLOCK), seq, I32)

    @pl.when(jnp.max(n_ge) > topk)
    def _():
        x = jnp.zeros((1, Q_BLOCK), I32)
        for bit in range(int(np.log2(seq)) - 1, -1, -1):
            cand = x + np.int32(1 << bit)
            hc = count(lambda v, p, cand=cand: jnp.where(v == thr, jnp.where(p < cand, 1, 0), 0))
            x = jnp.where(hc < need, cand, x)
        xp_ref[...] = jnp.broadcast_to(x + 1, (8, Q_BLOCK))

    xp = xp_ref[0:1, :]

    def bias_chunk(c, carry):
        k0 = pl.multiple_of(c * ck, ck)
        v = ikey_ref[pl.ds(k0, ck), :]
        s_pos = k0 + lax.broadcasted_iota(I32, (ck, Q_BLOCK), 0)
        tie = jnp.where(s_pos < xp, 0.0, NEG)
        b = jnp.where(v > thr, 0.0, jnp.where(v == thr, tie, NEG))
        b = jnp.where(s_pos <= t_pos, b, NEG).astype(F32)
        bias_ref[:, pl.ds(k0, ck)] = b.T
        return carry

    lax.fori_loop(0, nkc, bias_chunk, 0)

    zpad = jnp.zeros((Q_BLOCK, 2 * KV_RANK - KV_RANK - D_ROPE), BF16)
    for h in range(N_HEADS_A):
        qall_ref[h * Q_BLOCK:(h + 1) * Q_BLOCK, :] = jnp.concatenate(
            [qlat_ref[:, h * KV_RANK:(h + 1) * KV_RANK], qrope_ref[:, h * D_ROPE:(h + 1) * D_ROPE], zpad], axis=1)
    rows = N_HEADS_A * Q_BLOCK
    m_ref[...] = jnp.full((rows, 1), NEG, F32)
    l_ref[...] = jnp.zeros((rows, 1), F32)
    acc_ref[...] = jnp.zeros((rows, KV_RANK), F32)

    c2 = scale * np.log2(np.e)
    hg = DSA_HEAD_GROUP
    grows = hg * Q_BLOCK

    def attn_chunk(c, carry):
        k0 = pl.multiple_of(c * ck, ck)
        kt = kcatT_ref[:, pl.ds(k0, ck)]
        cv = ckv_ref[pl.ds(k0, ck), :]
        bias = bias_ref[:, pl.ds(k0, ck)]
        for g in range(N_HEADS_A // hg):
            rs = slice(g * grows, (g + 1) * grows)
            s = _dot(qall_ref[rs, :], kt) * c2
            s = (s.reshape(hg, Q_BLOCK, ck) + bias[None]).reshape(grows, ck)
            m_old = m_ref[rs, :]
            m_new = jnp.maximum(m_old, jnp.max(s, axis=1, keepdims=True))
            alpha = jnp.exp2(m_old - m_new)
            p = jnp.exp2(s - m_new)
            l_ref[rs, :] = alpha * l_ref[rs, :] + jnp.sum(p, axis=1, keepdims=True)
            acc_ref[rs, :] = alpha * acc_ref[rs, :] + _dot(p.astype(BF16), cv)
            m_ref[rs, :] = m_new
        return carry

    lax.fori_loop(0, nkc, attn_chunk, 0)

    o = (acc_ref[...] / l_ref[...]).astype(BF16)
    o_lat = jnp.concatenate([o[h * Q_BLOCK:(h + 1) * Q_BLOCK, :] for h in range(N_HEADS_A)], axis=1)
    out_ref[...] = _dot(o_lat, wuv_ref[...]).astype(BF16)


def _dsa(qlat, qrope, qidx_h, weffT, kidx, kcatT, ckv, wuv_bd, seq):
    b = qlat.shape[0]
    nb = seq // Q_BLOCK
    topk = min(TOPK_MAX, seq // 4)
    blkmap = lambda i, j: (i, j, 0)
    seqmap = lambda i, j: (i, 0, 0)
    rows = N_HEADS_A * Q_BLOCK
    return pl.pallas_call(
        functools.partial(_dsa_kernel, seq=seq, topk=topk),
        grid=(b, nb),
        in_specs=[pl.BlockSpec((None, Q_BLOCK, N_HEADS_A * KV_RANK), blkmap),
                  pl.BlockSpec((None, Q_BLOCK, N_HEADS_A * D_ROPE), blkmap),
                  pl.BlockSpec((None, IDX_HEADS, Q_BLOCK, IDX_DIM), lambda i, j: (i, 0, j, 0)),
                  pl.BlockSpec((None, IDX_HEADS, Q_BLOCK), lambda i, j: (i, 0, j)),
                  pl.BlockSpec((None, seq, IDX_DIM), seqmap),
                  pl.BlockSpec((None, 2 * KV_RANK, seq), seqmap),
                  pl.BlockSpec((None, seq, KV_RANK), seqmap),
                  pl.BlockSpec((N_HEADS_A * KV_RANK, N_HEADS_A * V_DIM), lambda i, j: (0, 0))],
        out_specs=pl.BlockSpec((None, Q_BLOCK, N_HEADS_A * V_DIM), blkmap),
        out_shape=jax.ShapeDtypeStruct((b, seq, N_HEADS_A * V_DIM), BF16),
        scratch_shapes=[pltpu.VMEM((seq, Q_BLOCK), I32),
                        pltpu.VMEM((Q_BLOCK, seq), F32),
                        pltpu.VMEM((rows, 2 * KV_RANK), BF16),
                        pltpu.VMEM((8, Q_BLOCK), I32),
                        pltpu.VMEM((rows, 1), F32),
                        pltpu.VMEM((rows, 1), F32),
                        pltpu.VMEM((rows, KV_RANK), F32)],
        compiler_params=pltpu.CompilerParams(dimension_semantics=("arbitrary", "arbitrary"),
                                             vmem_limit_bytes=VMEM_LIMIT),
        name="dsa",
    )(qlat, qrope, qidx_h, weffT, kidx, kcatT, ckv, wuv_bd)


DILATED_PATTERNS = ((128, 1), (512, 4), (2048, 16))
DIL_GROUP = 4


def _dilated_kernel(q_ref, k_ref, v_ref, out_ref, acc_ref, m_ref, l_ref, band_ref, tri_ref, *, seq):
    qb = Q_BLOCK
    lane = lax.broadcasted_iota(I32, (1, LANES), 1)
    head0 = lane < HEAD_DIM
    scale = HEAD_DIM ** -0.5

    a2 = lax.broadcasted_iota(I32, (qb, 2 * qb), 0)
    c2 = lax.broadcasted_iota(I32, (qb, 2 * qb), 1)
    band_ref[...] = jnp.where(c2 >= a2, jnp.where(c2 <= a2 + qb, 0.0, NEG), NEG)
    a1 = lax.broadcasted_iota(I32, (qb, qb), 0)
    c1 = lax.broadcasted_iota(I32, (qb, qb), 1)
    tri_ref[...] = jnp.where(c1 <= a1, 0.0, NEG)

    def rows(ref, start, stride):
        if stride == 1:
            return ref[pl.ds(start, qb), :]
        return ref[pl.ds(start, qb, stride=stride), :]

    def put(ref, start, stride, val):
        if stride == 1:
            ref[pl.ds(start, qb), :] = val
        else:
            ref[pl.ds(start, qb, stride=stride), :] = val

    def load_kv(start, stride):
        kk = rows(k_ref, start, stride)
        return (jnp.where(head0, kk, 0.0).astype(BF16), jnp.where(head0, 0.0, kk).astype(BF16),
                rows(v_ref, start, stride).astype(BF16))

    def attend(q_start, stride, kvs, first):
        q = (rows(q_ref, q_start, stride) * scale).astype(BF16)
        cat = (lambda xs: xs[0]) if len(kvs) == 1 else (lambda xs: jnp.concatenate(xs, axis=0))
        bias = tri_ref[...] if len(kvs) == 1 else band_ref[...]
        vv = cat([kv[2] for kv in kvs])
        pvs, ms, ls = [], [], []
        for hh in range(2):
            s = _dot_nt(q, cat([kv[hh] for kv in kvs])) + bias
            m = jnp.max(s, axis=1, keepdims=True)
            p = jnp.exp(s - m)
            ls.append(jnp.sum(p, axis=1, keepdims=True))
            ms.append(m)
            pvs.append(_dot(p.astype(BF16), vv))
        pv = jnp.where(head0, pvs[0], pvs[1])
        mb = jnp.where(head0, ms[0], ms[1])
        lb = jnp.where(head0, ls[0], ls[1])
        if first:
            put(acc_ref, q_start, stride, pv)
            put(m_ref, q_start, stride, mb)
            put(l_ref, q_start, stride, lb)
        else:
            mo = rows(m_ref, q_start, stride)
            mn = jnp.maximum(mo, mb)
            a_old = jnp.exp(mo - mn)
            a_new = jnp.exp(mb - mn)
            put(acc_ref, q_start, stride, rows(acc_ref, q_start, stride) * a_old + pv * a_new)
            put(l_ref, q_start, stride, rows(l_ref, q_start, stride) * a_old + lb * a_new)
            put(m_ref, q_start, stride, mn)

    def chain(base, stride, nblk, prev_start, first):
        step = qb * stride
        kvs = [load_kv(base + u * step, stride) for u in range(nblk)]
        prev = None if prev_start is None else load_kv(prev_start, stride)
        for u in range(nblk):
            before = kvs[u - 1] if u > 0 else prev
            attend(base + u * step, stride, [kvs[u]] if before is None else [before, kvs[u]], first)

    g = DIL_GROUP
    first = True
    for window, d in DILATED_PATTERNS:
        n = seq // d
        assert window // d == qb and n % qb == 0
        nb = n // qb
        if nb == 1:
            def body(j, carry, d=d, first=first):
                for u in range(g):
                    chain(j * g + u, d, 1, None, first)
                return carry
            lax.fori_loop(0, d // g, body, 0)
        elif nb <= g:
            def body(r, carry, d=d, nb=nb, first=first):
                chain(r, d, nb, None, first)
                return carry
            lax.fori_loop(0, d, body, 0)
        else:
            assert d == 1 and nb % g == 0
            chain(0, d, g, None, first)

            def body(j, carry, d=d, first=first):
                base = pl.multiple_of(j * (g * qb), g * qb)
                chain(base, d, g, base - qb, first)
                return carry
            lax.fori_loop(1, nb // g, body, 0)
        first = False

    out_ref[...] = (acc_ref[...] / l_ref[...]).astype(BF16)


def _dilated(qb, kb, vb, seq):
    b = qb.shape[0]
    npair = N_HEADS_B * HEAD_DIM // LANES
    spec = pl.BlockSpec((None, seq, LANES), lambda i, j: (i, 0, j))
    return pl.pallas_call(
        functools.partial(_dilated_kernel, seq=seq),
        grid=(b, npair),
        in_specs=[spec, spec, spec],
        out_specs=spec,
        out_shape=jax.ShapeDtypeStruct((b, seq, N_HEADS_B * HEAD_DIM), BF16),
        scratch_shapes=[pltpu.VMEM((seq, LANES), F32)] * 3 + [pltpu.VMEM((Q_BLOCK, 2 * Q_BLOCK), F32),
                                                             pltpu.VMEM((Q_BLOCK, Q_BLOCK), F32)],
        compiler_params=pltpu.CompilerParams(dimension_semantics=("arbitrary", "arbitrary"),
                                             vmem_limit_bytes=VMEM_LIMIT),
        name="dilated",
    )(qb, kb, vb)


def _memkv_kernel(mem_ref, g_ref, w_ref, k_ref, v_ref):
    m = _rms(mem_ref[...], g_ref[...]).astype(BF16)
    kv = _dot(m, w_ref[...])
    k_ref[...] = kv[:, :D_MODEL].astype(BF16)
    v_ref[...] = kv[:, D_MODEL:].astype(BF16)


def _memkv(mem2, g, w_kv):
    n = mem2.shape[0]
    tm = MEM_TOKENS
    row = lambda i: (i, 0)
    const = lambda i: (0, 0)
    return pl.pallas_call(
        _memkv_kernel,
        grid=(n // tm,),
        in_specs=[pl.BlockSpec((tm, D_MODEL), row), pl.BlockSpec((1, D_MODEL), const),
                  pl.BlockSpec((D_MODEL, 2 * D_MODEL), const)],
        out_specs=[pl.BlockSpec((tm, D_MODEL), row)] * 2,
        out_shape=[jax.ShapeDtypeStruct((n, D_MODEL), BF16)] * 2,
        compiler_params=pltpu.CompilerParams(dimension_semantics=("arbitrary",), vmem_limit_bytes=VMEM_LIMIT),
        name="memkv",
    )(mem2, g, w_kv)


def _cross_kernel(x_ref, oa_ref, ob_ref, wout_ref, g_ref, wq_ref, kc_ref, vc_ref, wo_ref, out_ref):
    half = N_HEADS_A * V_DIM
    x1 = x_ref[...] + _dot(oa_ref[...], wout_ref[:half, :]) + _dot(ob_ref[...], wout_ref[half:, :])
    qc = _dot(_rms(x1, g_ref[...]).astype(BF16), wq_ref[...]).astype(BF16)
    scale = CROSS_HEAD_DIM ** -0.5
    ocs = []
    for h in range(CROSS_HEADS):
        sl = slice(h * CROSS_HEAD_DIM, (h + 1) * CROSS_HEAD_DIM)
        s = _dot_nt(qc[:, sl], kc_ref[:, sl]) * scale
        p = jnp.exp(s - jnp.max(s, axis=1, keepdims=True))
        p = p / jnp.sum(p, axis=1, keepdims=True)
        ocs.append(_dot(p.astype(BF16), vc_ref[:, sl]).astype(BF16))
    oc = jnp.concatenate(ocs, axis=1)
    out_ref[...] = x1 + _dot(oc, wo_ref[...])


def _cross(x2, oa, ob, w_out, g, w_q, kc, vc, w_o, seq, tm):
    n = x2.shape[0]
    per_seq = seq // tm
    row = lambda i: (i, 0)
    const = lambda i: (0, 0)
    memmap = lambda i: (i // per_seq, 0)
    half = N_HEADS_A * V_DIM
    return pl.pallas_call(
        _cross_kernel,
        grid=(n // tm,),
        in_specs=[pl.BlockSpec((tm, D_MODEL), row), pl.BlockSpec((tm, half), row), pl.BlockSpec((tm, half), row),
                  pl.BlockSpec((D_MODEL, D_MODEL), const), pl.BlockSpec((1, D_MODEL), const),
                  pl.BlockSpec((D_MODEL, D_MODEL), const),
                  pl.BlockSpec((MEM_TOKENS, D_MODEL), memmap), pl.BlockSpec((MEM_TOKENS, D_MODEL), memmap),
                  pl.BlockSpec((D_MODEL, D_MODEL), const)],
        out_specs=pl.BlockSpec((tm, D_MODEL), row),
        out_shape=jax.ShapeDtypeStruct((n, D_MODEL), F32),
        compiler_params=pltpu.CompilerParams(dimension_semantics=("arbitrary",), vmem_limit_bytes=VMEM_LIMIT),
        name="cross",
    )(x2, oa, ob, w_out, g, w_q, kc, vc, w_o)


MLP_FF_CHUNK = 1024


def _mlp_kernel(x_ref, g_ref, wup_ref, wdown_ref, gf_ref, out_ref):
    x = x_ref[...]
    hm = _rms(x, g_ref[...]).astype(BF16)
    y = x
    for c in range(D_FF // MLP_FF_CHUNK):
        sl = slice(c * MLP_FF_CHUNK, (c + 1) * MLP_FF_CHUNK)
        u = jnp.maximum(_dot(hm, wup_ref[:, sl]), 0.0)
        y = y + _dot((u * u).astype(BF16), wdown_ref[sl, :])
    out_ref[...] = _rms(y, gf_ref[...])


def _mlp(x2, g, w_up, w_down, gf, tm):
    n = x2.shape[0]
    row = lambda i: (i, 0)
    const = lambda i: (0, 0)
    return pl.pallas_call(
        _mlp_kernel,
        grid=(n // tm,),
        in_specs=[pl.BlockSpec((tm, D_MODEL), row), pl.BlockSpec((1, D_MODEL), const),
                  pl.BlockSpec((D_MODEL, D_FF), const), pl.BlockSpec((D_FF, D_MODEL), const),
                  pl.BlockSpec((1, D_MODEL), const)],
        out_specs=pl.BlockSpec((tm, D_MODEL), row),
        out_shape=jax.ShapeDtypeStruct((n, D_MODEL), F32),
        compiler_params=pltpu.CompilerParams(dimension_semantics=("arbitrary",), vmem_limit_bytes=VMEM_LIMIT),
        name="mlp",
    )(x2, g, w_up, w_down, gf)


def _block_diag(w):
    h, a, b = w.shape
    eye = jnp.eye(h, dtype=w.dtype)
    return (eye[:, None, :, None] * w[:, :, None, :]).reshape(h * a, h * b)


def kernel(x, mem, norm_mix_g, w_in, kv_norm_g, w_uk, w_uv, w_out, norm_cross_g, norm_mem_g,
           w_q_cross, w_kv_cross, w_o_cross, norm_mlp_g, w_up, w_down, norm_final_g):
    b, seq, _ = x.shape
    assert seq == 2048 and w_in.shape[0] == 1, "kernel is specialised to SEQ=2048, DEPTH=1"
    tm = 512

    wi = w_in[0]
    col = lambda k: wi[:, _OFF[k]:_OFF[k + 1]]
    misc = jnp.concatenate([col(3), col(5), col(6), jnp.zeros((D_MODEL, LANES - 104), F32)], axis=1)
    w_cat = jnp.concatenate([col(0), col(1), col(2), misc, col(4), col(7)], axis=1).astype(BF16)
    wuk_bd = _block_diag(w_uk[0]).astype(BF16)
    wuv_bd = _block_diag(w_uv[0]).astype(BF16)
    tabs = _rope_tables(seq)

    x2 = x.reshape(b * seq, D_MODEL)
    qlat, qrope, ckv, miscp, qidx, qb, kb, vb = _inproj(
        x2, norm_mix_g[0][None], w_cat, wuk_bd, kv_norm_g[0][None], tabs, seq, tm)

    miscp = miscp.reshape(b, seq, LANES)
    krope = miscp[:, :, MISC_KR:MISC_KR + D_ROPE].astype(BF16)
    kidx = miscp[:, :, MISC_KI:MISC_KI + IDX_DIM].astype(BF16)
    weffT = jnp.swapaxes(miscp[:, :, MISC_WI:MISC_WI + IDX_HEADS], 1, 2)
    ckv = ckv.reshape(b, seq, KV_RANK)
    kcat = jnp.concatenate([ckv, krope, jnp.zeros((b, seq, 2 * KV_RANK - KV_RANK - D_ROPE), BF16)], axis=2)
    kcatT = jnp.swapaxes(kcat, 1, 2)
    qidx_h = jnp.swapaxes(qidx.reshape(b, seq, IDX_HEADS, IDX_DIM), 1, 2)

    o_a = _dsa(qlat.reshape(b, seq, -1), qrope.reshape(b, seq, -1), qidx_h, weffT, kidx, kcatT, ckv, wuv_bd, seq)
    o_b = _dilated(qb.reshape(b, seq, -1), kb.reshape(b, seq, -1), vb.reshape(b, seq, -1), seq)

    kc, vc = _memkv(mem.reshape(b * MEM_TOKENS, D_MODEL), norm_mem_g[0][None], w_kv_cross[0].astype(BF16))
    xc = _cross(x2, o_a.reshape(b * seq, -1), o_b.reshape(b * seq, -1), w_out[0].astype(BF16),
                norm_cross_g[0][None], w_q_cross[0].astype(BF16), kc, vc, w_o_cross[0].astype(BF16), seq, tm)
    out = _mlp(xc, norm_mlp_g[0][None], w_up[0].astype(BF16), w_down[0].astype(BF16), norm_final_g[None], tm)
    return out.reshape(b, seq, D_MODEL)
```

```python
import functools

import numpy as np
import jax
import jax.numpy as jnp
from jax import lax
from jax.experimental import pallas as pl
from jax.experimental.pallas import tpu as pltpu

F32 = jnp.float32
BF16 = jnp.bfloat16
I32 = jnp.int32

D_MODEL = 1024
HEAD_DIM = 64
N_HEADS_A = 8
N_HEADS_B = 8
D_NOPE = 64
D_ROPE = 32
KV_RANK = 128
V_DIM = 64
IDX_HEADS = 8
IDX_DIM = 64
TOPK_MAX = 256
Q_BLOCK = 128
CROSS_HEADS = 4
CROSS_HEAD_DIM = 256
MEM_TOKENS = 256
D_FF = 4096
ROPE_THETA = 10000.0
NORM_EPS = 1e-6

LANES = 128
VMEM_LIMIT = 48 * 1024 * 1024
NEG = -1e30
INT_MIN = -2 ** 31

_OFF = np.cumsum([0, 512, 256, 128, 32, 512, 64, 8, 1536])
C_QN, C_QR, C_CKV, C_MISC, C_QI, C_QB, C_KB, C_VB, C_END = 0, 512, 768, 896, 1024, 1536, 2048, 2560, 3072
MISC_KR, MISC_KI, MISC_WI = 0, 32, 96


def _dot(a, b):
    return jnp.dot(a, b, preferred_element_type=F32)


def _dot_nt(a, b):
    return lax.dot_general(a, b, (((1,), (1,)), ((), ())), preferred_element_type=F32)


def _rms(x, g):
    return x * lax.rsqrt(jnp.mean(x * x, axis=-1, keepdims=True) + NORM_EPS) * g


def _rope_tables(seq):
    pos = jnp.arange(seq, dtype=F32)[:, None]
    lane = np.arange(LANES)

    def tables(d, lanes_local, active):
        half = d // 2
        inv = ROPE_THETA ** (-jnp.arange(0, d, 2, dtype=F32) / d)
        ang = pos * inv[None, :]
        cos, sin = jnp.cos(ang), jnp.sin(ang)
        f = (lanes_local % d) % half
        first = jnp.asarray(((lanes_local % d) < half) & active)[None, :]
        second = jnp.asarray(((lanes_local % d) >= half) & active)[None, :]
        act = jnp.asarray(active)[None, :]
        c = jnp.where(act, cos[:, f], 0.0)
        sa = jnp.where(first, -sin[:, f], 0.0)
        sb = jnp.where(second, sin[:, f], 0.0)
        return c, sa, sb

    all_on = np.ones(LANES, bool)
    c64, sa64, sb64 = tables(64, lane, all_on)
    c32, sa32, sb32 = tables(32, lane, all_on)
    kr_on = lane < MISC_KI
    ki_on = (lane >= MISC_KI) & (lane < MISC_WI)
    ckr, sa16m, sb16m = tables(32, lane, kr_on)
    cki, sa32m, sb32m = tables(64, lane - MISC_KI, ki_on)
    w_scale = (IDX_HEADS ** -0.5) * (IDX_DIM ** -0.5)
    wi_on = jnp.asarray((lane >= MISC_WI) & (lane < MISC_WI + IDX_HEADS))[None, :]
    cosm = ckr + cki + jnp.where(wi_on, w_scale, 0.0)
    return jnp.stack([c64, sa64, sb64, c32, sa32, sb32, cosm, sa16m, sb16m, sa32m, sb32m], axis=0)


def _rope_lanes(x, cos, sa, sb, half):
    outs = []
    for c in range(x.shape[1] // LANES):
        xs = x[:, c * LANES:(c + 1) * LANES]
        outs.append(xs * cos + pltpu.roll(xs, LANES - half, 1) * sa + pltpu.roll(xs, half, 1) * sb)
    return outs[0] if len(outs) == 1 else jnp.concatenate(outs, axis=1)


def _inproj_kernel(x_ref, g_ref, w_ref, wuk_ref, kvg_ref, tab_ref,
                   qlat_ref, qrope_ref, ckv_ref, misc_ref, qidx_ref, qb_ref, kb_ref, vb_ref):
    h = _rms(x_ref[...], g_ref[...]).astype(BF16)

    def proj(c0, c1):
        return _dot(h, w_ref[:, c0:c1])

    c64, sa64, sb64 = tab_ref[0], tab_ref[1], tab_ref[2]
    c32, sa32, sb32 = tab_ref[3], tab_ref[4], tab_ref[5]

    qn = proj(C_QN, C_QR).astype(BF16)
    qlat_ref[...] = _dot(qn, wuk_ref[...]).astype(BF16)
    qrope_ref[...] = _rope_lanes(proj(C_QR, C_CKV), c32, sa32, sb32, D_ROPE // 2).astype(BF16)
    ckv_ref[...] = _rms(proj(C_CKV, C_MISC), kvg_ref[...]).astype(BF16)
    pm = proj(C_MISC, C_QI)
    misc_ref[...] = (pm * tab_ref[6]
                     + pltpu.roll(pm, LANES - 16, 1) * tab_ref[7] + pltpu.roll(pm, 16, 1) * tab_ref[8]
                     + pltpu.roll(pm, LANES - 32, 1) * tab_ref[9] + pltpu.roll(pm, 32, 1) * tab_ref[10])
    qidx_ref[...] = _rope_lanes(proj(C_QI, C_QB), c64, sa64, sb64, IDX_DIM // 2).astype(BF16)
    qb_ref[...] = _rope_lanes(proj(C_QB, C_KB), c64, sa64, sb64, HEAD_DIM // 2)
    kb_ref[...] = _rope_lanes(proj(C_KB, C_VB), c64, sa64, sb64, HEAD_DIM // 2)
    vb_ref[...] = proj(C_VB, C_END)


def _inproj(x2, g, w_cat, wuk_bd, kvg, tabs, seq, tm):
    n = x2.shape[0]
    per_seq = seq // tm
    row = lambda i: (i, 0)
    const = lambda i: (0, 0)
    outs = [(D_MODEL, BF16), (N_HEADS_A * D_ROPE, BF16), (KV_RANK, BF16), (LANES, F32),
            (IDX_HEADS * IDX_DIM, BF16), (512, F32), (512, F32), (512, F32)]
    return pl.pallas_call(
        _inproj_kernel,
        grid=(n // tm,),
        in_specs=[pl.BlockSpec((tm, D_MODEL), row),
                  pl.BlockSpec((1, D_MODEL), const),
                  pl.BlockSpec((D_MODEL, C_END), const),
                  pl.BlockSpec((N_HEADS_A * D_NOPE, N_HEADS_A * KV_RANK), const),
                  pl.BlockSpec((1, KV_RANK), const),
                  pl.BlockSpec((11, tm, LANES), lambda i: (0, i % per_seq, 0))],
        out_specs=[pl.BlockSpec((tm, w), row) for w, _ in outs],
        out_shape=[jax.ShapeDtypeStruct((n, w), dt) for w, dt in outs],
        compiler_params=pltpu.CompilerParams(dimension_semantics=("arbitrary",), vmem_limit_bytes=VMEM_LIMIT),
        name="inproj",
    )(x2, g, w_cat, wuk_bd, kvg, tabs)


DSA_CK = 512
DSA_HEAD_GROUP = 2


def _dsa_kernel(qlat_ref, qrope_ref, qidx_ref, weff_ref, kidx_ref, kcatT_ref, ckv_ref, wuv_ref,
                out_ref, ikey_ref, bias_ref, qall_ref, xp_ref, m_ref, l_ref, acc_ref, *, seq, topk):
    blk = pl.program_id(1)
    q0 = blk * Q_BLOCK
    nkc = (q0 + Q_BLOCK + DSA_CK - 1) // DSA_CK
    ck = DSA_CK
    scale = (D_NOPE + D_ROPE) ** -0.5

    qi_all = qidx_ref[...].reshape(IDX_HEADS * Q_BLOCK, IDX_DIM)
    t_pos = q0 + lax.broadcasted_iota(I32, (ck, Q_BLOCK), 1)

    def index_chunk(c, carry):
        k0 = pl.multiple_of(c * ck, ck)
        kc = kidx_ref[pl.ds(k0, ck), :]
        idx = jnp.zeros((ck, Q_BLOCK), F32)
        for g in range(IDX_HEADS // 2):
            lg = _dot_nt(kc, qi_all[2 * g * Q_BLOCK:(2 * g + 2) * Q_BLOCK, :])
            for hh in range(2):
                h = 2 * g + hh
                idx = idx + jnp.maximum(lg[:, hh * Q_BLOCK:(hh + 1) * Q_BLOCK], 0.0) * weff_ref[h:h + 1, :]
        s_pos = k0 + lax.broadcasted_iota(I32, (ck, Q_BLOCK), 0)
        idx = jnp.where(idx == 0.0, 0.0, idx)
        idx = jnp.where(s_pos <= t_pos, idx, -jnp.inf)
        bits = pltpu.bitcast(idx, I32)
        ikey_ref[pl.ds(k0, ck), :] = bits ^ ((bits >> 31) & 0x7FFFFFFF)
        return carry

    lax.fori_loop(0, nkc, index_chunk, 0)

    def count(pred):
        def body(c, cnt):
            k0 = pl.multiple_of(c * ck, ck)
            v = ikey_ref[pl.ds(k0, ck), :]
            s_pos = k0 + lax.broadcasted_iota(I32, (ck, Q_BLOCK), 0)
            ind = pred(v, s_pos)
            return cnt + jnp.sum(ind.reshape(ck // 8, 8, Q_BLOCK), axis=0)
        cnt8 = lax.fori_loop(0, nkc, body, jnp.zeros((8, Q_BLOCK), I32))
        return jnp.sum(cnt8, axis=0, keepdims=True)

    thr = jnp.full((1, Q_BLOCK), INT_MIN, I32)
    for bit in range(31, -1, -1):
        cand = jnp.zeros((1, Q_BLOCK), I32) if bit == 31 else thr + np.int32(1 << bit)
        cnt = count(lambda v, p, cand=cand: jnp.where(v >= cand, 1, 0))
        thr = jnp.where(cnt >= topk, cand, thr)

    need = topk - count(lambda v, p: jnp.where(v > thr, 1, 0))
    n_ge = count(lambda v, p: jnp.where(v >= thr, 1, 0))
    xp_ref[...] = jnp.full((8, Q_BLOCK), seq, I32)

    @pl.when(jnp.max(n_ge) > topk)
    def _():
        x = jnp.zeros((1, Q_BLOCK), I32)
        for bit in range(int(np.log2(seq)) - 1, -1, -1):
            cand = x + np.int32(1 << bit)
            hc = count(lambda v, p, cand=cand: jnp.where(v == thr, jnp.where(p < cand, 1, 0), 0))
            x = jnp.where(hc < need, cand, x)
        xp_ref[...] = jnp.broadcast_to(x + 1, (8, Q_BLOCK))

    xp = xp_ref[0:1, :]

    def bias_chunk(c, carry):
        k0 = pl.multiple_of(c * ck, ck)
        v = ikey_ref[pl.ds(k0, ck), :]
        s_pos = k0 + lax.broadcasted_iota(I32, (ck, Q_BLOCK), 0)
        tie = jnp.where(s_pos < xp, 0.0, NEG)
        b = jnp.where(v > thr, 0.0, jnp.where(v == thr, tie, NEG))
        b = jnp.where(s_pos <= t_pos, b, NEG).astype(F32)
        bias_ref[:, pl.ds(k0, ck)] = b.T
        return carry

    lax.fori_loop(0, nkc, bias_chunk, 0)

    zpad = jnp.zeros((Q_BLOCK, 2 * KV_RANK - KV_RANK - D_ROPE), BF16)
    for h in range(N_HEADS_A):
        qall_ref[h * Q_BLOCK:(h + 1) * Q_BLOCK, :] = jnp.concatenate(
            [qlat_ref[:, h * KV_RANK:(h + 1) * KV_RANK], qrope_ref[:, h * D_ROPE:(h + 1) * D_ROPE], zpad], axis=1)
    rows = N_HEADS_A * Q_BLOCK
    m_ref[...] = jnp.full((rows, LANES), NEG, F32)
    l_ref[...] = jnp.zeros((rows, LANES), F32)
    acc_ref[...] = jnp.zeros((rows, KV_RANK), F32)

    c2 = scale * np.log2(np.e)
    hg = DSA_HEAD_GROUP
    grows = hg * Q_BLOCK

    def attn_chunk(c, carry):
        k0 = pl.multiple_of(c * ck, ck)
        kt = kcatT_ref[:, pl.ds(k0, ck)]
        cv = ckv_ref[pl.ds(k0, ck), :]
        bias = bias_ref[:, pl.ds(k0, ck)]
        groups = [slice(g * grows, (g + 1) * grows) for g in range(N_HEADS_A // hg)]
        olds = [(m_ref[rs, :], l_ref[rs, :], acc_ref[rs, :]) for rs in groups]
        news = []
        for rs, (m_old, l_old, acc_old) in zip(groups, olds):
            s = _dot(qall_ref[rs, :], kt) * c2
            s = (s.reshape(hg, Q_BLOCK, ck) + bias[None]).reshape(grows, ck)
            m_new = jnp.maximum(m_old, jnp.max(s, axis=1, keepdims=True))
            alpha = jnp.exp2(m_old - m_new)
            p = jnp.exp2(s - pltpu.repeat(m_new, ck // LANES, axis=1))
            news.append((m_new, alpha * l_old + jnp.sum(p, axis=1, keepdims=True),
                         alpha * acc_old + _dot(p.astype(BF16), cv)))
        for rs, (m_new, l_new, acc_new) in zip(groups, news):
            m_ref[rs, :] = m_new
            l_ref[rs, :] = l_new
            acc_ref[rs, :] = acc_new
        return carry

    lax.fori_loop(0, nkc, attn_chunk, 0)

    o = (acc_ref[...] / l_ref[...]).astype(BF16)
    o_lat = jnp.concatenate([o[h * Q_BLOCK:(h + 1) * Q_BLOCK, :] for h in range(N_HEADS_A)], axis=1)
    out_ref[...] = _dot(o_lat, wuv_ref[...]).astype(BF16)


def _dsa(qlat, qrope, qidx_h, weffT, kidx, kcatT, ckv, wuv_bd, seq):
    b = qlat.shape[0]
    nb = seq // Q_BLOCK
    topk = min(TOPK_MAX, seq // 4)
    blkmap = lambda i, j: (i, j, 0)
    seqmap = lambda i, j: (i, 0, 0)
    rows = N_HEADS_A * Q_BLOCK
    return pl.pallas_call(
        functools.partial(_dsa_kernel, seq=seq, topk=topk),
        grid=(b, nb),
        in_specs=[pl.BlockSpec((None, Q_BLOCK, N_HEADS_A * KV_RANK), blkmap),
                  pl.BlockSpec((None, Q_BLOCK, N_HEADS_A * D_ROPE), blkmap),
                  pl.BlockSpec((None, IDX_HEADS, Q_BLOCK, IDX_DIM), lambda i, j: (i, 0, j, 0)),
                  pl.BlockSpec((None, IDX_HEADS, Q_BLOCK), lambda i, j: (i, 0, j)),
                  pl.BlockSpec((None, seq, IDX_DIM), seqmap),
                  pl.BlockSpec((None, 2 * KV_RANK, seq), seqmap),
                  pl.BlockSpec((None, seq, KV_RANK), seqmap),
                  pl.BlockSpec((N_HEADS_A * KV_RANK, N_HEADS_A * V_DIM), lambda i, j: (0, 0))],
        out_specs=pl.BlockSpec((None, Q_BLOCK, N_HEADS_A * V_DIM), blkmap),
        out_shape=jax.ShapeDtypeStruct((b, seq, N_HEADS_A * V_DIM), BF16),
        scratch_shapes=[pltpu.VMEM((seq, Q_BLOCK), I32),
                        pltpu.VMEM((Q_BLOCK, seq), F32),
                        pltpu.VMEM((rows, 2 * KV_RANK), BF16),
                        pltpu.VMEM((8, Q_BLOCK), I32),
                        pltpu.VMEM((rows, LANES), F32),
                        pltpu.VMEM((rows, LANES), F32),
                        pltpu.VMEM((rows, KV_RANK), F32)],
        compiler_params=pltpu.CompilerParams(dimension_semantics=("arbitrary", "arbitrary"),
                                             vmem_limit_bytes=VMEM_LIMIT),
        name="dsa",
    )(qlat, qrope, qidx_h, weffT, kidx, kcatT, ckv, wuv_bd)


DILATED_PATTERNS = ((128, 1), (512, 4), (2048, 16))
DIL_GROUP = 4


def _dilated_kernel(q_ref, k_ref, v_ref, out_ref, acc_ref, m_ref, l_ref, acc4_ref, m4_ref, l4_ref,
                    q4_ref, k4_ref, v4_ref, q16_ref, k16_ref, v16_ref, band_ref, tri_ref, *, seq):
    qb = Q_BLOCK
    lane = lax.broadcasted_iota(I32, (1, LANES), 1)
    head0 = lane < HEAD_DIM
    scale = HEAD_DIM ** -0.5

    a2 = lax.broadcasted_iota(I32, (qb, 2 * qb), 0)
    c2 = lax.broadcasted_iota(I32, (qb, 2 * qb), 1)
    band_ref[...] = jnp.where(c2 >= a2, jnp.where(c2 <= a2 + qb, 0.0, NEG), NEG)
    a1 = lax.broadcasted_iota(I32, (qb, qb), 0)
    c1 = lax.broadcasted_iota(I32, (qb, qb), 1)
    tri_ref[...] = jnp.where(c1 <= a1, 0.0, NEG)

    def rows(ref, start, stride=1):
        if stride == 1:
            return ref[pl.ds(start, qb), :]
        return ref[pl.ds(start, qb, stride=stride), :]

    def put(ref, start, stride, val):
        if stride == 1:
            ref[pl.ds(start, qb), :] = val
        else:
            ref[pl.ds(start, qb, stride=stride), :] = val

    n4, n16 = seq // 4, seq // 16
    for src, m4, m16 in ((q_ref, q4_ref, q16_ref), (k_ref, k4_ref, k16_ref), (v_ref, v4_ref, v16_ref)):
        def to_m4(c, carry, src=src, m4=m4):
            for r4 in range(4):
                m4[pl.ds(r4 * n4 + c * qb, qb), :] = rows(src, c * (4 * qb) + r4, 4)
            return carry
        lax.fori_loop(0, n4 // qb, to_m4, 0)

        def to_m16(r4, carry, m4=m4, m16=m16):
            for j in range(4):
                m16[pl.ds((r4 + 4 * j) * n16, qb), :] = rows(m4, r4 * n4 + j, 4)
            return carry
        lax.fori_loop(0, 4, to_m16, 0)

    def load_kv(kv_refs, start):
        kk = rows(kv_refs[0], start)
        return (jnp.where(head0, kk, 0.0).astype(BF16), jnp.where(head0, 0.0, kk).astype(BF16),
                rows(kv_refs[1], start).astype(BF16))

    def cat(xs):
        return xs[0] if len(xs) == 1 else jnp.concatenate(xs, axis=0)

    def attend_all(qs, kv_lists):
        scores = [[_dot_nt(q, cat([kv[hh] for kv in kvs])) for hh in range(2)] for q, kvs in zip(qs, kv_lists)]
        probs = []
        for ss, kvs in zip(scores, kv_lists):
            bias = tri_ref[...] if len(kvs) == 1 else band_ref[...]
            row = []
            for s in ss:
                s = s + bias
                m = jnp.max(s, axis=1, keepdims=True)
                p = jnp.exp(s - m)
                row.append((p.astype(BF16), m, jnp.sum(p, axis=1, keepdims=True)))
            probs.append(row)
        outs = []
        for row, kvs in zip(probs, kv_lists):
            vv = cat([kv[2] for kv in kvs])
            pvs = [_dot(p, vv) for p, _, _ in row]
            outs.append((jnp.where(head0, pvs[0], pvs[1]), jnp.where(head0, row[0][1], row[1][1]),
                         jnp.where(head0, row[0][2], row[1][2])))
        return outs

    def merged(old, new):
        mn = jnp.maximum(old[1], new[1])
        a_old = jnp.exp(old[1] - mn)
        a_new = jnp.exp(new[1] - mn)
        return old[0] * a_old + new[0] * a_new, mn, old[2] * a_old + new[2] * a_new

    def run_group(srcs, blocks, probs, stats_in, store):
        kvs = [load_kv(srcs[1:], s) for s in blocks]
        qs = [(rows(srcs[0], q_start) * scale).astype(BF16) for q_start, _ in probs]
        olds = [None if stats_in is None else tuple(rows(r, q_start) for r in stats_in) for q_start, _ in probs]
        news = attend_all(qs, [[kvs[b] for b in kb] for _, kb in probs])
        for i, (new, old) in enumerate(zip(news, olds)):
            store(i, new if old is None else merged(old, new))

    def chain_probs(blocks, key_only_first):
        return [(blocks[u], [u - 1, u] if u > 0 else [u]) for u in range(1 if key_only_first else 0, len(blocks))]

    assert DILATED_PATTERNS == ((qb, 1), (4 * qb, 4), (16 * qb, 16)) and n16 == qb
    g = DIL_GROUP
    stats4 = (acc4_ref, m4_ref, l4_ref)
    stats = (acc_ref, m_ref, l_ref)

    def store_to(refs, starts):
        def store(i, vals):
            for ref, val in zip(refs, vals):
                put(ref, starts[i], 1, val)
        return store

    def p16_body(j, carry):
        blocks = [(r4 + 4 * j) * n16 for r4 in range(4)]

        def store(r4, vals):
            for ref, val in zip(stats4, vals):
                put(ref, r4 * n4 + j, 4, val)
        run_group((q16_ref, k16_ref, v16_ref), blocks, [(blocks[r4], [r4]) for r4 in range(4)], None, store)
        return carry
    lax.fori_loop(0, 4, p16_body, 0)

    def p4_body(r4, carry):
        base = pl.multiple_of(r4 * n4, n4)
        blocks = [base + u * qb for u in range(n4 // qb)]
        run_group((q4_ref, k4_ref, v4_ref), blocks, chain_probs(blocks, False), stats4, store_to(stats4, blocks))
        return carry
    lax.fori_loop(0, 4, p4_body, 0)

    def to_nat(c, carry):
        for r4 in range(4):
            for s4, s1 in zip(stats4, stats):
                put(s1, c * (4 * qb) + r4, 4, rows(s4, r4 * n4 + c * qb))
        return carry
    lax.fori_loop(0, n4 // qb, to_nat, 0)

    nb = seq // qb
    assert nb % g == 0
    nat = (q_ref, k_ref, v_ref)
    blocks0 = [u * qb for u in range(g)]
    run_group(nat, blocks0, chain_probs(blocks0, False), stats, store_to(stats, blocks0))

    def p1_body(i, carry):
        base = pl.multiple_of(i * (g * qb), g * qb)
        blocks = [base + (u - 1) * qb for u in range(g + 1)]
        run_group(nat, blocks, chain_probs(blocks, True), stats, store_to(stats, blocks[1:]))
        return carry
    lax.fori_loop(1, nb // g, p1_body, 0)

    out_ref[...] = (acc_ref[...] / l_ref[...]).astype(BF16)


def _dilated(qb, kb, vb, seq):
    b = qb.shape[0]
    npair = N_HEADS_B * HEAD_DIM // LANES
    spec = pl.BlockSpec((None, seq, LANES), lambda i, j: (i, 0, j))
    return pl.pallas_call(
        functools.partial(_dilated_kernel, seq=seq),
        grid=(b, npair),
        in_specs=[spec, spec, spec],
        out_specs=spec,
        out_shape=jax.ShapeDtypeStruct((b, seq, N_HEADS_B * HEAD_DIM), BF16),
        scratch_shapes=[pltpu.VMEM((seq, LANES), F32)] * 12 + [pltpu.VMEM((Q_BLOCK, 2 * Q_BLOCK), F32),
                                                              pltpu.VMEM((Q_BLOCK, Q_BLOCK), F32)],
        compiler_params=pltpu.CompilerParams(dimension_semantics=("arbitrary", "arbitrary"),
                                             vmem_limit_bytes=VMEM_LIMIT),
        name="dilated",
    )(qb, kb, vb)


def _memkv_kernel(mem_ref, g_ref, w_ref, k_ref, v_ref):
    m = _rms(mem_ref[...], g_ref[...]).astype(BF16)
    kv = _dot(m, w_ref[...])
    k_ref[...] = kv[:, :D_MODEL].astype(BF16)
    v_ref[...] = kv[:, D_MODEL:].astype(BF16)


def _memkv(mem2, g, w_kv):
    n = mem2.shape[0]
    tm = MEM_TOKENS
    row = lambda i: (i, 0)
    const = lambda i: (0, 0)
    return pl.pallas_call(
        _memkv_kernel,
        grid=(n // tm,),
        in_specs=[pl.BlockSpec((tm, D_MODEL), row), pl.BlockSpec((1, D_MODEL), const),
                  pl.BlockSpec((D_MODEL, 2 * D_MODEL), const)],
        out_specs=[pl.BlockSpec((tm, D_MODEL), row)] * 2,
        out_shape=[jax.ShapeDtypeStruct((n, D_MODEL), BF16)] * 2,
        compiler_params=pltpu.CompilerParams(dimension_semantics=("arbitrary",), vmem_limit_bytes=VMEM_LIMIT),
        name="memkv",
    )(mem2, g, w_kv)


def _cross_kernel(x_ref, oa_ref, ob_ref, wout_ref, g_ref, wq_ref, kc_ref, vc_ref, wo_ref, out_ref):
    half = N_HEADS_A * V_DIM
    x1 = x_ref[...] + _dot(oa_ref[...], wout_ref[:half, :]) + _dot(ob_ref[...], wout_ref[half:, :])
    qc = _dot(_rms(x1, g_ref[...]).astype(BF16), wq_ref[...]).astype(BF16)
    scale = CROSS_HEAD_DIM ** -0.5
    ocs = []
    for h in range(CROSS_HEADS):
        sl = slice(h * CROSS_HEAD_DIM, (h + 1) * CROSS_HEAD_DIM)
        s = _dot_nt(qc[:, sl], kc_ref[:, sl]) * scale
        p = jnp.exp(s - jnp.max(s, axis=1, keepdims=True))
        p = p / jnp.sum(p, axis=1, keepdims=True)
        ocs.append(_dot(p.astype(BF16), vc_ref[:, sl]).astype(BF16))
    oc = jnp.concatenate(ocs, axis=1)
    out_ref[...] = x1 + _dot(oc, wo_ref[...])


def _cross(x2, oa, ob, w_out, g, w_q, kc, vc, w_o, seq, tm):
    n = x2.shape[0]
    per_seq = seq // tm
    row = lambda i: (i, 0)
    const = lambda i: (0, 0)
    memmap = lambda i: (i // per_seq, 0)
    half = N_HEADS_A * V_DIM
    return pl.pallas_call(
        _cross_kernel,
        grid=(n // tm,),
        in_specs=[pl.BlockSpec((tm, D_MODEL), row), pl.BlockSpec((tm, half), row), pl.BlockSpec((tm, half), row),
                  pl.BlockSpec((D_MODEL, D_MODEL), const), pl.BlockSpec((1, D_MODEL), const),
                  pl.BlockSpec((D_MODEL, D_MODEL), const),
                  pl.BlockSpec((MEM_TOKENS, D_MODEL), memmap), pl.BlockSpec((MEM_TOKENS, D_MODEL), memmap),
                  pl.BlockSpec((D_MODEL, D_MODEL), const)],
        out_specs=pl.BlockSpec((tm, D_MODEL), row),
        out_shape=jax.ShapeDtypeStruct((n, D_MODEL), F32),
        compiler_params=pltpu.CompilerParams(dimension_semantics=("arbitrary",), vmem_limit_bytes=VMEM_LIMIT),
        name="cross",
    )(x2, oa, ob, w_out, g, w_q, kc, vc, w_o)


MLP_FF_CHUNK = 1024


def _mlp_kernel(x_ref, g_ref, wup_ref, wdown_ref, gf_ref, out_ref):
    x = x_ref[...]
    hm = _rms(x, g_ref[...]).astype(BF16)
    y = x
    for c in range(D_FF // MLP_FF_CHUNK):
        sl = slice(c * MLP_FF_CHUNK, (c + 1) * MLP_FF_CHUNK)
        u = jnp.maximum(_dot(hm, wup_ref[:, sl]), 0.0)
        y = y + _dot((u * u).astype(BF16), wdown_ref[sl, :])
    out_ref[...] = _rms(y, gf_ref[...])


def _mlp(x2, g, w_up, w_down, gf, tm):
    n = x2.shape[0]
    row = lambda i: (i, 0)
    const = lambda i: (0, 0)
    return pl.pallas_call(
        _mlp_kernel,
        grid=(n // tm,),
        in_specs=[pl.BlockSpec((tm, D_MODEL), row), pl.BlockSpec((1, D_MODEL), const),
                  pl.BlockSpec((D_MODEL, D_FF), const), pl.BlockSpec((D_FF, D_MODEL), const),
                  pl.BlockSpec((1, D_MODEL), const)],
        out_specs=pl.BlockSpec((tm, D_MODEL), row),
        out_shape=jax.ShapeDtypeStruct((n, D_MODEL), F32),
        compiler_params=pltpu.CompilerParams(dimension_semantics=("arbitrary",), vmem_limit_bytes=VMEM_LIMIT),
        name="mlp",
    )(x2, g, w_up, w_down, gf)


def _block_diag(w):
    h, a, b = w.shape
    eye = jnp.eye(h, dtype=w.dtype)
    return (eye[:, None, :, None] * w[:, :, None, :]).reshape(h * a, h * b)


def kernel(x, mem, norm_mix_g, w_in, kv_norm_g, w_uk, w_uv, w_out, norm_cross_g, norm_mem_g,
           w_q_cross, w_kv_cross, w_o_cross, norm_mlp_g, w_up, w_down, norm_final_g):
    b, seq, _ = x.shape
    assert seq == 2048 and w_in.shape[0] == 1, "kernel is specialised to SEQ=2048, DEPTH=1"
    tm = 512

    wi = w_in[0]
    col = lambda k: wi[:, _OFF[k]:_OFF[k + 1]]
    misc = jnp.concatenate([col(3), col(5), col(6), jnp.zeros((D_MODEL, LANES - 104), F32)], axis=1)
    w_cat = jnp.concatenate([col(0), col(1), col(2), misc, col(4), col(7)], axis=1).astype(BF16)
    wuk_bd = _block_diag(w_uk[0]).astype(BF16)
    wuv_bd = _block_diag(w_uv[0]).astype(BF16)
    tabs = _rope_tables(seq)

    x2 = x.reshape(b * seq, D_MODEL)
    qlat, qrope, ckv, miscp, qidx, qb, kb, vb = _inproj(
        x2, norm_mix_g[0][None], w_cat, wuk_bd, kv_norm_g[0][None], tabs, seq, tm)

    miscp = miscp.reshape(b, seq, LANES)
    krope = miscp[:, :, MISC_KR:MISC_KR + D_ROPE].astype(BF16)
    kidx = miscp[:, :, MISC_KI:MISC_KI + IDX_DIM].astype(BF16)
    weffT = jnp.swapaxes(miscp[:, :, MISC_WI:MISC_WI + IDX_HEADS], 1, 2)
    ckv = ckv.reshape(b, seq, KV_RANK)
    kcat = jnp.concatenate([ckv, krope, jnp.zeros((b, seq, 2 * KV_RANK - KV_RANK - D_ROPE), BF16)], axis=2)
    kcatT = jnp.swapaxes(kcat, 1, 2)
    qidx_h = jnp.swapaxes(qidx.reshape(b, seq, IDX_HEADS, IDX_DIM), 1, 2)

    o_a = _dsa(qlat.reshape(b, seq, -1), qrope.reshape(b, seq, -1), qidx_h, weffT, kidx, kcatT, ckv, wuv_bd, seq)
    o_b = _dilated(qb.reshape(b, seq, -1), kb.reshape(b, seq, -1), vb.reshape(b, seq, -1), seq)

    kc, vc = _memkv(mem.reshape(b * MEM_TOKENS, D_MODEL), norm_mem_g[0][None], w_kv_cross[0].astype(BF16))
    xc = _cross(x2, o_a.reshape(b * seq, -1), o_b.reshape(b * seq, -1), w_out[0].astype(BF16),
                norm_cross_g[0][None], w_q_cross[0].astype(BF16), kc, vc, w_o_cross[0].astype(BF16), seq, tm)
    out = _mlp(xc, norm_mlp_g[0][None], w_up[0].astype(BF16), w_down[0].astype(BF16), norm_final_g[None], tm)
    return out.reshape(b, seq, D_MODEL)
```

```python
import functools

import numpy as np
import jax
import jax.numpy as jnp
from jax import lax
from jax.experimental import pallas as pl
from jax.experimental.pallas import tpu as pltpu

F32 = jnp.float32
BF16 = jnp.bfloat16
I32 = jnp.int32

D_MODEL = 1024
HEAD_DIM = 64
N_HEADS_A = 8
N_HEADS_B = 8
D_NOPE = 64
D_ROPE = 32
KV_RANK = 128
V_DIM = 64
IDX_HEADS = 8
IDX_DIM = 64
TOPK_MAX = 256
Q_BLOCK = 128
CROSS_HEADS = 4
CROSS_HEAD_DIM = 256
MEM_TOKENS = 256
D_FF = 4096
ROPE_THETA = 10000.0
NORM_EPS = 1e-6

LANES = 128
VMEM_LIMIT = 48 * 1024 * 1024
NEG = -1e30
INT_MIN = -2 ** 31
KEY_NEG_INF = -2139095041

_OFF = np.cumsum([0, 512, 256, 128, 32, 512, 64, 8, 1536])
C_QN, C_QR, C_CKV, C_MISC, C_QI, C_QB, C_KB, C_VB, C_END = 0, 512, 768, 896, 1024, 1536, 2048, 2560, 3072
MISC_KR, MISC_KI, MISC_WI = 0, 32, 96


def _dot(a, b):
    return jnp.dot(a, b, preferred_element_type=F32)


def _dot_nt(a, b):
    return lax.dot_general(a, b, (((1,), (1,)), ((), ())), preferred_element_type=F32)


def _rms(x, g):
    return x * lax.rsqrt(jnp.mean(x * x, axis=-1, keepdims=True) + NORM_EPS) * g


def _rope_tables(seq):
    pos = jnp.arange(seq, dtype=F32)[:, None]
    lane = np.arange(LANES)

    def tables(d, lanes_local, active):
        half = d // 2
        inv = ROPE_THETA ** (-jnp.arange(0, d, 2, dtype=F32) / d)
        ang = pos * inv[None, :]
        cos, sin = jnp.cos(ang), jnp.sin(ang)
        f = (lanes_local % d) % half
        first = jnp.asarray(((lanes_local % d) < half) & active)[None, :]
        second = jnp.asarray(((lanes_local % d) >= half) & active)[None, :]
        act = jnp.asarray(active)[None, :]
        c = jnp.where(act, cos[:, f], 0.0)
        sa = jnp.where(first, -sin[:, f], 0.0)
        sb = jnp.where(second, sin[:, f], 0.0)
        return c, sa, sb

    all_on = np.ones(LANES, bool)
    c64, sa64, sb64 = tables(64, lane, all_on)
    c32, sa32, sb32 = tables(32, lane, all_on)
    kr_on = lane < MISC_KI
    ki_on = (lane >= MISC_KI) & (lane < MISC_WI)
    ckr, sa16m, sb16m = tables(32, lane, kr_on)
    cki, sa32m, sb32m = tables(64, lane - MISC_KI, ki_on)
    w_scale = (IDX_HEADS ** -0.5) * (IDX_DIM ** -0.5)
    wi_on = jnp.asarray((lane >= MISC_WI) & (lane < MISC_WI + IDX_HEADS))[None, :]
    cosm = ckr + cki + jnp.where(wi_on, w_scale, 0.0)
    return jnp.stack([c64, sa64, sb64, c32, sa32, sb32, cosm, sa16m, sb16m, sa32m, sb32m], axis=0)


def _rope_lanes(x, cos, sa, sb, half):
    outs = []
    for c in range(x.shape[1] // LANES):
        xs = x[:, c * LANES:(c + 1) * LANES]
        outs.append(xs * cos + pltpu.roll(xs, LANES - half, 1) * sa + pltpu.roll(xs, half, 1) * sb)
    return outs[0] if len(outs) == 1 else jnp.concatenate(outs, axis=1)


def _inproj_kernel(x_ref, g_ref, w_ref, wuk_ref, kvg_ref, tab_ref,
                   qlat_ref, qrope_ref, ckv_ref, misc_ref, qidx_ref, qb_ref, kb_ref, vb_ref):
    h = _rms(x_ref[...], g_ref[...]).astype(BF16)

    def proj(c0, c1):
        return _dot(h, w_ref[:, c0:c1])

    c64, sa64, sb64 = tab_ref[0], tab_ref[1], tab_ref[2]
    c32, sa32, sb32 = tab_ref[3], tab_ref[4], tab_ref[5]

    qn = proj(C_QN, C_QR).astype(BF16)
    qlat_ref[...] = _dot(qn, wuk_ref[...]).astype(BF16)
    qrope_ref[...] = _rope_lanes(proj(C_QR, C_CKV), c32, sa32, sb32, D_ROPE // 2).astype(BF16)
    ckv_ref[...] = _rms(proj(C_CKV, C_MISC), kvg_ref[...]).astype(BF16)
    pm = proj(C_MISC, C_QI)
    misc_ref[...] = (pm * tab_ref[6]
                     + pltpu.roll(pm, LANES - 16, 1) * tab_ref[7] + pltpu.roll(pm, 16, 1) * tab_ref[8]
                     + pltpu.roll(pm, LANES - 32, 1) * tab_ref[9] + pltpu.roll(pm, 32, 1) * tab_ref[10])
    qidx_ref[...] = _rope_lanes(proj(C_QI, C_QB), c64, sa64, sb64, IDX_DIM // 2).astype(BF16)
    qb_ref[...] = _rope_lanes(proj(C_QB, C_KB), c64, sa64, sb64, HEAD_DIM // 2)
    kb_ref[...] = _rope_lanes(proj(C_KB, C_VB), c64, sa64, sb64, HEAD_DIM // 2)
    vb_ref[...] = proj(C_VB, C_END)


def _inproj(x2, g, w_cat, wuk_bd, kvg, tabs, seq, tm):
    n = x2.shape[0]
    per_seq = seq // tm
    row = lambda i: (i, 0)
    const = lambda i: (0, 0)
    outs = [(D_MODEL, BF16), (N_HEADS_A * D_ROPE, BF16), (KV_RANK, BF16), (LANES, F32),
            (IDX_HEADS * IDX_DIM, BF16), (512, F32), (512, F32), (512, F32)]
    return pl.pallas_call(
        _inproj_kernel,
        grid=(n // tm,),
        in_specs=[pl.BlockSpec((tm, D_MODEL), row),
                  pl.BlockSpec((1, D_MODEL), const),
                  pl.BlockSpec((D_MODEL, C_END), const),
                  pl.BlockSpec((N_HEADS_A * D_NOPE, N_HEADS_A * KV_RANK), const),
                  pl.BlockSpec((1, KV_RANK), const),
                  pl.BlockSpec((11, tm, LANES), lambda i: (0, i % per_seq, 0))],
        out_specs=[pl.BlockSpec((tm, w), row) for w, _ in outs],
        out_shape=[jax.ShapeDtypeStruct((n, w), dt) for w, dt in outs],
        compiler_params=pltpu.CompilerParams(dimension_semantics=("arbitrary",), vmem_limit_bytes=VMEM_LIMIT),
        name="inproj",
    )(x2, g, w_cat, wuk_bd, kvg, tabs)


DSA_CK = 512
DSA_HEAD_GROUP = 2


def _dsa_kernel(qlat_ref, qrope_ref, qidx_ref, weff_ref, kidx_ref, kcatT_ref, ckv_ref, wuv_ref,
                out_ref, ikey_ref, planes_ref, bias_ref, qall_ref, xp_ref, m_ref, l_ref, acc_ref, *, seq, topk):
    blk = pl.program_id(1)
    q0 = blk * Q_BLOCK
    nkc = (q0 + Q_BLOCK + DSA_CK - 1) // DSA_CK
    ck = DSA_CK
    scale = (D_NOPE + D_ROPE) ** -0.5

    qi_all = qidx_ref[...].reshape(IDX_HEADS * Q_BLOCK, IDX_DIM)
    t_pos = q0 + lax.broadcasted_iota(I32, (ck, Q_BLOCK), 1)

    def index_chunk(c, carry):
        k0 = pl.multiple_of(c * ck, ck)
        kc = kidx_ref[pl.ds(k0, ck), :]
        idx = jnp.zeros((ck, Q_BLOCK), F32)
        for g in range(IDX_HEADS // 2):
            lg = _dot_nt(kc, qi_all[2 * g * Q_BLOCK:(2 * g + 2) * Q_BLOCK, :])
            for hh in range(2):
                h = 2 * g + hh
                idx = idx + jnp.maximum(lg[:, hh * Q_BLOCK:(hh + 1) * Q_BLOCK], 0.0) * weff_ref[h:h + 1, :]
        s_pos = k0 + lax.broadcasted_iota(I32, (ck, Q_BLOCK), 0)
        idx = jnp.where(idx == 0.0, 0.0, idx)
        idx = jnp.where(s_pos <= t_pos, idx, -jnp.inf)
        bits = pltpu.bitcast(idx, I32)
        ikey_ref[pl.ds(k0, ck), :] = bits ^ ((bits >> 31) & 0x7FFFFFFF)
        return carry

    lax.fori_loop(0, nkc, index_chunk, 0)

    def count(pred):
        def body(c, cnt):
            k0 = pl.multiple_of(c * ck, ck)
            v = ikey_ref[pl.ds(k0, ck), :]
            s_pos = k0 + lax.broadcasted_iota(I32, (ck, Q_BLOCK), 0)
            ind = pred(v, s_pos)
            return cnt + jnp.sum(ind.reshape(ck // 8, 8, Q_BLOCK), axis=0)
        cnt8 = lax.fori_loop(0, nkc, body, jnp.zeros((8, Q_BLOCK), I32))
        return jnp.sum(cnt8, axis=0, keepdims=True)

    def fill_chunk(c, carry):
        ikey_ref[pl.ds(pl.multiple_of(c * ck, ck), ck), :] = jnp.full((ck, Q_BLOCK), KEY_NEG_INF, I32)
        return carry
    lax.fori_loop(nkc, seq // ck, fill_chunk, 0)

    grp = seq // 32
    assert grp % 8 == 0

    def bit_transpose(v, carry):
        off = pl.multiple_of(v * 8, 8)
        a = [ikey_ref[pl.ds(grp * j + off, 8), :] for j in range(32)]
        j, msk = 16, 0x0000FFFF
        while j:
            k = 0
            while k < 32:
                t = (a[k] ^ lax.shift_right_logical(a[k + j], np.int32(j))) & np.int32(msk)
                a[k] = a[k] ^ t
                a[k + j] = a[k + j] ^ (t << np.int32(j))
                k = (k + j + 1) & ~j
            j >>= 1
            msk = (msk ^ (msk << j)) & 0xFFFFFFFF if j else msk
        a[0] = ~a[0]
        for i in range(32):
            planes_ref[pl.ds(grp * i + off, 8), :] = a[i]
        return carry
    lax.fori_loop(0, grp // 8, bit_transpose, 0)

    alive = jnp.full((grp, Q_BLOCK), -1, I32)
    n_gt = jnp.zeros((1, Q_BLOCK), I32)
    thr_u = jnp.zeros((1, Q_BLOCK), I32)
    for i in range(32):
        x = alive & planes_ref[grp * i:grp * (i + 1), :]
        c1 = jnp.sum(lax.population_count(x), axis=0, keepdims=True)
        take = (n_gt + c1) >= topk
        alive = jnp.where(take, x, alive ^ x)
        n_gt = jnp.where(take, n_gt, n_gt + c1)
        thr_u = jnp.where(take, thr_u | np.int32(-2 ** 31 if i == 0 else 1 << (31 - i)), thr_u)
    thr = thr_u ^ np.int32(INT_MIN)

    need = topk - n_gt
    n_ge = n_gt + jnp.sum(lax.population_count(alive), axis=0, keepdims=True)
    xp_ref[...] = jnp.full((8, Q_BLOCK), seq, I32)

    @pl.when(jnp.max(n_ge) > topk)
    def _():
        x = jnp.zeros((1, Q_BLOCK), I32)
        for bit in range(int(np.log2(seq)) - 1, -1, -1):
            cand = x + np.int32(1 << bit)
            hc = count(lambda v, p, cand=cand: jnp.where(v == thr, jnp.where(p < cand, 1, 0), 0))
            x = jnp.where(hc < need, cand, x)
        xp_ref[...] = jnp.broadcast_to(x + 1, (8, Q_BLOCK))

    xp = xp_ref[0:1, :]

    def bias_chunk(c, carry):
        k0 = pl.multiple_of(c * ck, ck)
        v = ikey_ref[pl.ds(k0, ck), :]
        s_pos = k0 + lax.broadcasted_iota(I32, (ck, Q_BLOCK), 0)
        tie = jnp.where(s_pos < xp, 0.0, NEG)
        b = jnp.where(v > thr, 0.0, jnp.where(v == thr, tie, NEG))
        b = jnp.where(s_pos <= t_pos, b, NEG).astype(F32)
        bias_ref[:, pl.ds(k0, ck)] = b.T
        return carry

    lax.fori_loop(0, nkc, bias_chunk, 0)

    zpad = jnp.zeros((Q_BLOCK, 2 * KV_RANK - KV_RANK - D_ROPE), BF16)
    for h in range(N_HEADS_A):
        qall_ref[h * Q_BLOCK:(h + 1) * Q_BLOCK, :] = jnp.concatenate(
            [qlat_ref[:, h * KV_RANK:(h + 1) * KV_RANK], qrope_ref[:, h * D_ROPE:(h + 1) * D_ROPE], zpad], axis=1)
    rows = N_HEADS_A * Q_BLOCK
    m_ref[...] = jnp.full((rows, LANES), NEG, F32)
    l_ref[...] = jnp.zeros((rows, LANES), F32)
    acc_ref[...] = jnp.zeros((rows, KV_RANK), F32)

    c2 = scale * np.log2(np.e)
    hg = DSA_HEAD_GROUP
    grows = hg * Q_BLOCK

    def attn_chunk(c, carry):
        k0 = pl.multiple_of(c * ck, ck)
        kt = kcatT_ref[:, pl.ds(k0, ck)]
        cv = ckv_ref[pl.ds(k0, ck), :]
        bias = bias_ref[:, pl.ds(k0, ck)]
        groups = [slice(g * grows, (g + 1) * grows) for g in range(N_HEADS_A // hg)]
        olds = [(m_ref[rs, :], l_ref[rs, :], acc_ref[rs, :]) for rs in groups]
        news = []
        for rs, (m_old, l_old, acc_old) in zip(groups, olds):
            s = _dot(qall_ref[rs, :], kt) * c2
            s = (s.reshape(hg, Q_BLOCK, ck) + bias[None]).reshape(grows, ck)
            m_new = jnp.maximum(m_old, jnp.max(s, axis=1, keepdims=True))
            alpha = jnp.exp2(m_old - m_new)
            p = jnp.exp2(s - pltpu.repeat(m_new, ck // LANES, axis=1))
            news.append((m_new, alpha * l_old + jnp.sum(p, axis=1, keepdims=True),
                         alpha * acc_old + _dot(p.astype(BF16), cv)))
        for rs, (m_new, l_new, acc_new) in zip(groups, news):
            m_ref[rs, :] = m_new
            l_ref[rs, :] = l_new
            acc_ref[rs, :] = acc_new
        return carry

    lax.fori_loop(0, nkc, attn_chunk, 0)

    o = (acc_ref[...] / l_ref[...]).astype(BF16)
    o_lat = jnp.concatenate([o[h * Q_BLOCK:(h + 1) * Q_BLOCK, :] for h in range(N_HEADS_A)], axis=1)
    out_ref[...] = _dot(o_lat, wuv_ref[...]).astype(BF16)


def _dsa(qlat, qrope, qidx_h, weffT, kidx, kcatT, ckv, wuv_bd, seq):
    b = qlat.shape[0]
    nb = seq // Q_BLOCK
    topk = min(TOPK_MAX, seq // 4)
    blkmap = lambda i, j: (i, j, 0)
    seqmap = lambda i, j: (i, 0, 0)
    rows = N_HEADS_A * Q_BLOCK
    return pl.pallas_call(
        functools.partial(_dsa_kernel, seq=seq, topk=topk),
        grid=(b, nb),
        in_specs=[pl.BlockSpec((None, Q_BLOCK, N_HEADS_A * KV_RANK), blkmap),
                  pl.BlockSpec((None, Q_BLOCK, N_HEADS_A * D_ROPE), blkmap),
                  pl.BlockSpec((None, IDX_HEADS, Q_BLOCK, IDX_DIM), lambda i, j: (i, 0, j, 0)),
                  pl.BlockSpec((None, IDX_HEADS, Q_BLOCK), lambda i, j: (i, 0, j)),
                  pl.BlockSpec((None, seq, IDX_DIM), seqmap),
                  pl.BlockSpec((None, 2 * KV_RANK, seq), seqmap),
                  pl.BlockSpec((None, seq, KV_RANK), seqmap),
                  pl.BlockSpec((N_HEADS_A * KV_RANK, N_HEADS_A * V_DIM), lambda i, j: (0, 0))],
        out_specs=pl.BlockSpec((None, Q_BLOCK, N_HEADS_A * V_DIM), blkmap),
        out_shape=jax.ShapeDtypeStruct((b, seq, N_HEADS_A * V_DIM), BF16),
        scratch_shapes=[pltpu.VMEM((seq, Q_BLOCK), I32),
                        pltpu.VMEM((seq, Q_BLOCK), I32),
                        pltpu.VMEM((Q_BLOCK, seq), F32),
                        pltpu.VMEM((rows, 2 * KV_RANK), BF16),
                        pltpu.VMEM((8, Q_BLOCK), I32),
                        pltpu.VMEM((rows, LANES), F32),
                        pltpu.VMEM((rows, LANES), F32),
                        pltpu.VMEM((rows, KV_RANK), F32)],
        compiler_params=pltpu.CompilerParams(dimension_semantics=("arbitrary", "arbitrary"),
                                             vmem_limit_bytes=VMEM_LIMIT),
        name="dsa",
    )(qlat, qrope, qidx_h, weffT, kidx, kcatT, ckv, wuv_bd)


DILATED_PATTERNS = ((128, 1), (512, 4), (2048, 16))
DIL_GROUP = 4


def _dilated_kernel(q_ref, k_ref, v_ref, out_ref, acc_ref, m_ref, l_ref, acc4_ref, m4_ref, l4_ref,
                    q4_ref, k4_ref, v4_ref, q16_ref, k16_ref, v16_ref, band_ref, tri_ref, *, seq):
    qb = Q_BLOCK
    lane = lax.broadcasted_iota(I32, (1, LANES), 1)
    head0 = lane < HEAD_DIM
    scale = HEAD_DIM ** -0.5

    a2 = lax.broadcasted_iota(I32, (qb, 2 * qb), 0)
    c2 = lax.broadcasted_iota(I32, (qb, 2 * qb), 1)
    band_ref[...] = jnp.where(c2 >= a2, jnp.where(c2 <= a2 + qb, 0.0, NEG), NEG)
    a1 = lax.broadcasted_iota(I32, (qb, qb), 0)
    c1 = lax.broadcasted_iota(I32, (qb, qb), 1)
    tri_ref[...] = jnp.where(c1 <= a1, 0.0, NEG)

    def rows(ref, start, stride=1):
        if stride == 1:
            return ref[pl.ds(start, qb), :]
        return ref[pl.ds(start, qb, stride=stride), :]

    def put(ref, start, stride, val):
        if stride == 1:
            ref[pl.ds(start, qb), :] = val
        else:
            ref[pl.ds(start, qb, stride=stride), :] = val

    n4, n16 = seq // 4, seq // 16
    for src, m4, m16 in ((q_ref, q4_ref, q16_ref), (k_ref, k4_ref, k16_ref), (v_ref, v4_ref, v16_ref)):
        def to_m4(c, carry, src=src, m4=m4):
            for r4 in range(4):
                m4[pl.ds(r4 * n4 + c * qb, qb), :] = rows(src, c * (4 * qb) + r4, 4)
            return carry
        lax.fori_loop(0, n4 // qb, to_m4, 0)

        def to_m16(r4, carry, m4=m4, m16=m16):
            for j in range(4):
                m16[pl.ds((r4 + 4 * j) * n16, qb), :] = rows(m4, r4 * n4 + j, 4)
            return carry
        lax.fori_loop(0, 4, to_m16, 0)

    def load_kv(kv_refs, start):
        kk = rows(kv_refs[0], start)
        return (jnp.where(head0, kk, 0.0).astype(BF16), jnp.where(head0, 0.0, kk).astype(BF16),
                rows(kv_refs[1], start).astype(BF16))

    def cat(xs):
        return xs[0] if len(xs) == 1 else jnp.concatenate(xs, axis=0)

    def attend_all(qs, kv_lists):
        scores = [[_dot_nt(q, cat([kv[hh] for kv in kvs])) for hh in range(2)] for q, kvs in zip(qs, kv_lists)]
        probs = []
        for ss, kvs in zip(scores, kv_lists):
            bias = tri_ref[...] if len(kvs) == 1 else band_ref[...]
            row = []
            for s in ss:
                s = s + bias
                m = jnp.max(s, axis=1, keepdims=True)
                p = jnp.exp(s - m)
                row.append((p.astype(BF16), m, jnp.sum(p, axis=1, keepdims=True)))
            probs.append(row)
        outs = []
        for row, kvs in zip(probs, kv_lists):
            vv = cat([kv[2] for kv in kvs])
            pvs = [_dot(p, vv) for p, _, _ in row]
            outs.append((jnp.where(head0, pvs[0], pvs[1]), jnp.where(head0, row[0][1], row[1][1]),
                         jnp.where(head0, row[0][2], row[1][2])))
        return outs

    def merged(old, new):
        mn = jnp.maximum(old[1], new[1])
        a_old = jnp.exp(old[1] - mn)
        a_new = jnp.exp(new[1] - mn)
        return old[0] * a_old + new[0] * a_new, mn, old[2] * a_old + new[2] * a_new

    def run_group(srcs, blocks, probs, stats_in, store):
        kvs = [load_kv(srcs[1:], s) for s in blocks]
        qs = [(rows(srcs[0], q_start) * scale).astype(BF16) for q_start, _ in probs]
        olds = [None if stats_in is None else tuple(rows(r, q_start) for r in stats_in) for q_start, _ in probs]
        news = attend_all(qs, [[kvs[b] for b in kb] for _, kb in probs])
        for i, (new, old) in enumerate(zip(news, olds)):
            store(i, new if old is None else merged(old, new))

    def chain_probs(blocks, key_only_first):
        return [(blocks[u], [u - 1, u] if u > 0 else [u]) for u in range(1 if key_only_first else 0, len(blocks))]

    assert DILATED_PATTERNS == ((qb, 1), (4 * qb, 4), (16 * qb, 16)) and n16 == qb
    g = DIL_GROUP
    stats4 = (acc4_ref, m4_ref, l4_ref)
    stats = (acc_ref, m_ref, l_ref)

    def store_to(refs, starts):
        def store(i, vals):
            for ref, val in zip(refs, vals):
                put(ref, starts[i], 1, val)
        return store

    def p16_body(j, carry):
        blocks = [(r4 + 4 * j) * n16 for r4 in range(4)]

        def store(r4, vals):
            for ref, val in zip(stats4, vals):
                put(ref, r4 * n4 + j, 4, val)
        run_group((q16_ref, k16_ref, v16_ref), blocks, [(blocks[r4], [r4]) for r4 in range(4)], None, store)
        return carry
    lax.fori_loop(0, 4, p16_body, 0)

    def p4_body(r4, carry):
        base = pl.multiple_of(r4 * n4, n4)
        blocks = [base + u * qb for u in range(n4 // qb)]
        run_group((q4_ref, k4_ref, v4_ref), blocks, chain_probs(blocks, False), stats4, store_to(stats4, blocks))
        return carry
    lax.fori_loop(0, 4, p4_body, 0)

    def to_nat(c, carry):
        for r4 in range(4):
            for s4, s1 in zip(stats4, stats):
                put(s1, c * (4 * qb) + r4, 4, rows(s4, r4 * n4 + c * qb))
        return carry
    lax.fori_loop(0, n4 // qb, to_nat, 0)

    nb = seq // qb
    assert nb % g == 0
    nat = (q_ref, k_ref, v_ref)
    blocks0 = [u * qb for u in range(g)]
    run_group(nat, blocks0, chain_probs(blocks0, False), stats, store_to(stats, blocks0))

    def p1_body(i, carry):
        base = pl.multiple_of(i * (g * qb), g * qb)
        blocks = [base + (u - 1) * qb for u in range(g + 1)]
        run_group(nat, blocks, chain_probs(blocks, True), stats, store_to(stats, blocks[1:]))
        return carry
    lax.fori_loop(1, nb // g, p1_body, 0)

    out_ref[...] = (acc_ref[...] / l_ref[...]).astype(BF16)


def _dilated(qb, kb, vb, seq):
    b = qb.shape[0]
    npair = N_HEADS_B * HEAD_DIM // LANES
    spec = pl.BlockSpec((None, seq, LANES), lambda i, j: (i, 0, j))
    return pl.pallas_call(
        functools.partial(_dilated_kernel, seq=seq),
        grid=(b, npair),
        in_specs=[spec, spec, spec],
        out_specs=spec,
        out_shape=jax.ShapeDtypeStruct((b, seq, N_HEADS_B * HEAD_DIM), BF16),
        scratch_shapes=[pltpu.VMEM((seq, LANES), F32)] * 12 + [pltpu.VMEM((Q_BLOCK, 2 * Q_BLOCK), F32),
                                                              pltpu.VMEM((Q_BLOCK, Q_BLOCK), F32)],
        compiler_params=pltpu.CompilerParams(dimension_semantics=("arbitrary", "arbitrary"),
                                             vmem_limit_bytes=VMEM_LIMIT),
        name="dilated",
    )(qb, kb, vb)


def _memkv_kernel(mem_ref, g_ref, w_ref, k_ref, v_ref):
    m = _rms(mem_ref[...], g_ref[...]).astype(BF16)
    kv = _dot(m, w_ref[...])
    k_ref[...] = kv[:, :D_MODEL].astype(BF16)
    v_ref[...] = kv[:, D_MODEL:].astype(BF16)


def _memkv(mem2, g, w_kv):
    n = mem2.shape[0]
    tm = MEM_TOKENS
    row = lambda i: (i, 0)
    const = lambda i: (0, 0)
    return pl.pallas_call(
        _memkv_kernel,
        grid=(n // tm,),
        in_specs=[pl.BlockSpec((tm, D_MODEL), row), pl.BlockSpec((1, D_MODEL), const),
                  pl.BlockSpec((D_MODEL, 2 * D_MODEL), const)],
        out_specs=[pl.BlockSpec((tm, D_MODEL), row)] * 2,
        out_shape=[jax.ShapeDtypeStruct((n, D_MODEL), BF16)] * 2,
        compiler_params=pltpu.CompilerParams(dimension_semantics=("arbitrary",), vmem_limit_bytes=VMEM_LIMIT),
        name="memkv",
    )(mem2, g, w_kv)


def _cross_kernel(x_ref, oa_ref, ob_ref, wout_ref, g_ref, wq_ref, kc_ref, vc_ref, wo_ref, out_ref):
    half = N_HEADS_A * V_DIM
    x1 = x_ref[...] + _dot(oa_ref[...], wout_ref[:half, :]) + _dot(ob_ref[...], wout_ref[half:, :])
    qc = _dot(_rms(x1, g_ref[...]).astype(BF16), wq_ref[...]).astype(BF16)
    scale = CROSS_HEAD_DIM ** -0.5
    ocs = []
    for h in range(CROSS_HEADS):
        sl = slice(h * CROSS_HEAD_DIM, (h + 1) * CROSS_HEAD_DIM)
        s = _dot_nt(qc[:, sl], kc_ref[:, sl]) * scale
        p = jnp.exp(s - jnp.max(s, axis=1, keepdims=True))
        p = p / jnp.sum(p, axis=1, keepdims=True)
        ocs.append(_dot(p.astype(BF16), vc_ref[:, sl]).astype(BF16))
    oc = jnp.concatenate(ocs, axis=1)
    out_ref[...] = x1 + _dot(oc, wo_ref[...])


def _cross(x2, oa, ob, w_out, g, w_q, kc, vc, w_o, seq, tm):
    n = x2.shape[0]
    per_seq = seq // tm
    row = lambda i: (i, 0)
    const = lambda i: (0, 0)
    memmap = lambda i: (i // per_seq, 0)
    half = N_HEADS_A * V_DIM
    return pl.pallas_call(
        _cross_kernel,
        grid=(n // tm,),
        in_specs=[pl.BlockSpec((tm, D_MODEL), row), pl.BlockSpec((tm, half), row), pl.BlockSpec((tm, half), row),
                  pl.BlockSpec((D_MODEL, D_MODEL), const), pl.BlockSpec((1, D_MODEL), const),
                  pl.BlockSpec((D_MODEL, D_MODEL), const),
                  pl.BlockSpec((MEM_TOKENS, D_MODEL), memmap), pl.BlockSpec((MEM_TOKENS, D_MODEL), memmap),
                  pl.BlockSpec((D_MODEL, D_MODEL), const)],
        out_specs=pl.BlockSpec((tm, D_MODEL), row),
        out_shape=jax.ShapeDtypeStruct((n, D_MODEL), F32),
        compiler_params=pltpu.CompilerParams(dimension_semantics=("arbitrary",), vmem_limit_bytes=VMEM_LIMIT),
        name="cross",
    )(x2, oa, ob, w_out, g, w_q, kc, vc, w_o)


MLP_FF_CHUNK = 1024


def _mlp_kernel(x_ref, g_ref, wup_ref, wdown_ref, gf_ref, out_ref):
    x = x_ref[...]
    hm = _rms(x, g_ref[...]).astype(BF16)
    y = x
    for c in range(D_FF // MLP_FF_CHUNK):
        sl = slice(c * MLP_FF_CHUNK, (c + 1) * MLP_FF_CHUNK)
        u = jnp.maximum(_dot(hm, wup_ref[:, sl]), 0.0)
        y = y + _dot((u * u).astype(BF16), wdown_ref[sl, :])
    out_ref[...] = _rms(y, gf_ref[...])


def _mlp(x2, g, w_up, w_down, gf, tm):
    n = x2.shape[0]
    row = lambda i: (i, 0)
    const = lambda i: (0, 0)
    return pl.pallas_call(
        _mlp_kernel,
        grid=(n // tm,),
        in_specs=[pl.BlockSpec((tm, D_MODEL), row), pl.BlockSpec((1, D_MODEL), const),
                  pl.BlockSpec((D_MODEL, D_FF), const), pl.BlockSpec((D_FF, D_MODEL), const),
                  pl.BlockSpec((1, D_MODEL), const)],
        out_specs=pl.BlockSpec((tm, D_MODEL), row),
        out_shape=jax.ShapeDtypeStruct((n, D_MODEL), F32),
        compiler_params=pltpu.CompilerParams(dimension_semantics=("arbitrary",), vmem_limit_bytes=VMEM_LIMIT),
        name="mlp",
    )(x2, g, w_up, w_down, gf)


def _block_diag(w):
    h, a, b = w.shape
    eye = jnp.eye(h, dtype=w.dtype)
    return (eye[:, None, :, None] * w[:, :, None, :]).reshape(h * a, h * b)


def kernel(x, mem, norm_mix_g, w_in, kv_norm_g, w_uk, w_uv, w_out, norm_cross_g, norm_mem_g,
           w_q_cross, w_kv_cross, w_o_cross, norm_mlp_g, w_up, w_down, norm_final_g):
    b, seq, _ = x.shape
    assert seq == 2048 and w_in.shape[0] == 1, "kernel is specialised to SEQ=2048, DEPTH=1"
    tm = 512

    wi = w_in[0]
    col = lambda k: wi[:, _OFF[k]:_OFF[k + 1]]
    misc = jnp.concatenate([col(3), col(5), col(6), jnp.zeros((D_MODEL, LANES - 104), F32)], axis=1)
    w_cat = jnp.concatenate([col(0), col(1), col(2), misc, col(4), col(7)], axis=1).astype(BF16)
    wuk_bd = _block_diag(w_uk[0]).astype(BF16)
    wuv_bd = _block_diag(w_uv[0]).astype(BF16)
    tabs = _rope_tables(seq)

    x2 = x.reshape(b * seq, D_MODEL)
    qlat, qrope, ckv, miscp, qidx, qb, kb, vb = _inproj(
        x2, norm_mix_g[0][None], w_cat, wuk_bd, kv_norm_g[0][None], tabs, seq, tm)

    miscp = miscp.reshape(b, seq, LANES)
    krope = miscp[:, :, MISC_KR:MISC_KR + D_ROPE].astype(BF16)
    kidx = miscp[:, :, MISC_KI:MISC_KI + IDX_DIM].astype(BF16)
    weffT = jnp.swapaxes(miscp[:, :, MISC_WI:MISC_WI + IDX_HEADS], 1, 2)
    ckv = ckv.reshape(b, seq, KV_RANK)
    kcat = jnp.concatenate([ckv, krope, jnp.zeros((b, seq, 2 * KV_RANK - KV_RANK - D_ROPE), BF16)], axis=2)
    kcatT = jnp.swapaxes(kcat, 1, 2)
    qidx_h = jnp.swapaxes(qidx.reshape(b, seq, IDX_HEADS, IDX_DIM), 1, 2)

    o_a = _dsa(qlat.reshape(b, seq, -1), qrope.reshape(b, seq, -1), qidx_h, weffT, kidx, kcatT, ckv, wuv_bd, seq)
    o_b = _dilated(qb.reshape(b, seq, -1), kb.reshape(b, seq, -1), vb.reshape(b, seq, -1), seq)

    kc, vc = _memkv(mem.reshape(b * MEM_TOKENS, D_MODEL), norm_mem_g[0][None], w_kv_cross[0].astype(BF16))
    xc = _cross(x2, o_a.reshape(b * seq, -1), o_b.reshape(b * seq, -1), w_out[0].astype(BF16),
                norm_cross_g[0][None], w_q_cross[0].astype(BF16), kc, vc, w_o_cross[0].astype(BF16), seq, tm)
    out = _mlp(xc, norm_mlp_g[0][None], w_up[0].astype(BF16), w_down[0].astype(BF16), norm_final_g[None], tm)
    return out.reshape(b, seq, D_MODEL)
```

```python
import functools

import numpy as np
import jax
import jax.numpy as jnp
from jax import lax
from jax.experimental import pallas as pl
from jax.experimental.pallas import tpu as pltpu

F32 = jnp.float32
BF16 = jnp.bfloat16
I32 = jnp.int32

D_MODEL = 1024
HEAD_DIM = 64
N_HEADS_A = 8
N_HEADS_B = 8
D_NOPE = 64
D_ROPE = 32
KV_RANK = 128
V_DIM = 64
IDX_HEADS = 8
IDX_DIM = 64
TOPK_MAX = 256
Q_BLOCK = 128
CROSS_HEADS = 4
CROSS_HEAD_DIM = 256
MEM_TOKENS = 256
D_FF = 4096
ROPE_THETA = 10000.0
NORM_EPS = 1e-6

LANES = 128
VMEM_LIMIT = 48 * 1024 * 1024
NEG = -1e30
INT_MIN = -2 ** 31
KEY_NEG_INF = -2139095041
DSA_Q_SCALE = float((D_NOPE + D_ROPE) ** -0.5 * np.log2(np.e))

_OFF = np.cumsum([0, 512, 256, 128, 32, 512, 64, 8, 1536])
C_QN, C_QR, C_CKV, C_MISC, C_QI, C_QB, C_KB, C_VB, C_END = 0, 512, 768, 896, 1024, 1536, 2048, 2560, 3072
MISC_KR, MISC_KI, MISC_WI = 0, 32, 96


def _dot(a, b):
    return jnp.dot(a, b, preferred_element_type=F32)


def _dot_nt(a, b):
    return lax.dot_general(a, b, (((1,), (1,)), ((), ())), preferred_element_type=F32)


def _rms(x, g):
    return x * lax.rsqrt(jnp.mean(x * x, axis=-1, keepdims=True) + NORM_EPS) * g


def _rope_tables(seq):
    pos = jnp.arange(seq, dtype=F32)[:, None]
    lane = np.arange(LANES)

    def tables(d, lanes_local, active):
        half = d // 2
        inv = ROPE_THETA ** (-jnp.arange(0, d, 2, dtype=F32) / d)
        ang = pos * inv[None, :]
        cos, sin = jnp.cos(ang), jnp.sin(ang)
        f = (lanes_local % d) % half
        first = jnp.asarray(((lanes_local % d) < half) & active)[None, :]
        second = jnp.asarray(((lanes_local % d) >= half) & active)[None, :]
        act = jnp.asarray(active)[None, :]
        c = jnp.where(act, cos[:, f], 0.0)
        sa = jnp.where(first, -sin[:, f], 0.0)
        sb = jnp.where(second, sin[:, f], 0.0)
        return c, sa, sb

    all_on = np.ones(LANES, bool)
    c64, sa64, sb64 = tables(64, lane, all_on)
    c32, sa32, sb32 = tables(32, lane, all_on)
    kr_on = lane < MISC_KI
    ki_on = (lane >= MISC_KI) & (lane < MISC_WI)
    ckr, sa16m, sb16m = tables(32, lane, kr_on)
    cki, sa32m, sb32m = tables(64, lane - MISC_KI, ki_on)
    w_scale = (IDX_HEADS ** -0.5) * (IDX_DIM ** -0.5)
    wi_on = jnp.asarray((lane >= MISC_WI) & (lane < MISC_WI + IDX_HEADS))[None, :]
    cosm = ckr + cki + jnp.where(wi_on, w_scale, 0.0)
    return jnp.stack([c64, sa64, sb64, c32, sa32, sb32, cosm, sa16m, sb16m, sa32m, sb32m], axis=0)


def _rope_lanes(x, cos, sa, sb, half):
    outs = []
    for c in range(x.shape[1] // LANES):
        xs = x[:, c * LANES:(c + 1) * LANES]
        outs.append(xs * cos + pltpu.roll(xs, LANES - half, 1) * sa + pltpu.roll(xs, half, 1) * sb)
    return outs[0] if len(outs) == 1 else jnp.concatenate(outs, axis=1)


def _inproj_kernel(x_ref, g_ref, w_ref, wuk_ref, kvg_ref, tab_ref,
                   qlat_ref, qrope_ref, ckv_ref, misc_ref, qidx_ref, qb_ref, kb_ref, vb_ref):
    h = _rms(x_ref[...], g_ref[...]).astype(BF16)

    def proj(c0, c1):
        return _dot(h, w_ref[:, c0:c1])

    c64, sa64, sb64 = tab_ref[0], tab_ref[1], tab_ref[2]
    c32, sa32, sb32 = tab_ref[3], tab_ref[4], tab_ref[5]

    qn = proj(C_QN, C_QR).astype(BF16)
    qlat_ref[...] = (_dot(qn, wuk_ref[...]) * DSA_Q_SCALE).astype(BF16)
    qrope_ref[...] = (_rope_lanes(proj(C_QR, C_CKV), c32, sa32, sb32, D_ROPE // 2) * DSA_Q_SCALE).astype(BF16)
    ckv_ref[...] = _rms(proj(C_CKV, C_MISC), kvg_ref[...]).astype(BF16)
    pm = proj(C_MISC, C_QI)
    misc_ref[...] = (pm * tab_ref[6]
                     + pltpu.roll(pm, LANES - 16, 1) * tab_ref[7] + pltpu.roll(pm, 16, 1) * tab_ref[8]
                     + pltpu.roll(pm, LANES - 32, 1) * tab_ref[9] + pltpu.roll(pm, 32, 1) * tab_ref[10])
    qidx_ref[...] = _rope_lanes(proj(C_QI, C_QB), c64, sa64, sb64, IDX_DIM // 2).astype(BF16)
    qb_ref[...] = _rope_lanes(proj(C_QB, C_KB), c64, sa64, sb64, HEAD_DIM // 2)
    kb_ref[...] = _rope_lanes(proj(C_KB, C_VB), c64, sa64, sb64, HEAD_DIM // 2)
    vb_ref[...] = proj(C_VB, C_END)


def _inproj(x2, g, w_cat, wuk_bd, kvg, tabs, seq, tm):
    n = x2.shape[0]
    per_seq = seq // tm
    row = lambda i: (i, 0)
    const = lambda i: (0, 0)
    outs = [(D_MODEL, BF16), (N_HEADS_A * D_ROPE, BF16), (KV_RANK, BF16), (LANES, F32),
            (IDX_HEADS * IDX_DIM, BF16), (512, F32), (512, F32), (512, F32)]
    return pl.pallas_call(
        _inproj_kernel,
        grid=(n // tm,),
        in_specs=[pl.BlockSpec((tm, D_MODEL), row),
                  pl.BlockSpec((1, D_MODEL), const),
                  pl.BlockSpec((D_MODEL, C_END), const),
                  pl.BlockSpec((N_HEADS_A * D_NOPE, N_HEADS_A * KV_RANK), const),
                  pl.BlockSpec((1, KV_RANK), const),
                  pl.BlockSpec((11, tm, LANES), lambda i: (0, i % per_seq, 0))],
        out_specs=[pl.BlockSpec((tm, w), row) for w, _ in outs],
        out_shape=[jax.ShapeDtypeStruct((n, w), dt) for w, dt in outs],
        compiler_params=pltpu.CompilerParams(dimension_semantics=("arbitrary",), vmem_limit_bytes=VMEM_LIMIT),
        name="inproj",
    )(x2, g, w_cat, wuk_bd, kvg, tabs)


DSA_CK = 512
DSA_ATT_CK = 512
DSA_HEAD_GROUP = 1


def _dsa_kernel(qlat_ref, qrope_ref, qidx_ref, weff_ref, kidx_ref, kcatT_ref, ckv_ref, wuv_ref,
                out_ref, ikey_ref, planes_ref, bias_ref, qall_ref, xp_ref, m_ref, acc_ref, pbuf_ref, *, seq, topk):
    blk = pl.program_id(1)
    q0 = blk * Q_BLOCK
    nkc = (q0 + Q_BLOCK + DSA_CK - 1) // DSA_CK
    ck = DSA_CK

    qi_all = qidx_ref[...].reshape(IDX_HEADS * Q_BLOCK, IDX_DIM)
    t_pos = q0 + lax.broadcasted_iota(I32, (ck, Q_BLOCK), 1)

    def index_chunk(c, carry):
        k0 = pl.multiple_of(c * ck, ck)
        kc = kidx_ref[pl.ds(k0, ck), :]
        idx = jnp.zeros((ck, Q_BLOCK), F32)
        for g in range(IDX_HEADS // 2):
            lg = _dot_nt(kc, qi_all[2 * g * Q_BLOCK:(2 * g + 2) * Q_BLOCK, :])
            for hh in range(2):
                h = 2 * g + hh
                idx = idx + jnp.maximum(lg[:, hh * Q_BLOCK:(hh + 1) * Q_BLOCK], 0.0) * weff_ref[h:h + 1, :]
        s_pos = k0 + lax.broadcasted_iota(I32, (ck, Q_BLOCK), 0)
        idx = jnp.where(idx == 0.0, 0.0, idx)
        idx = jnp.where(s_pos <= t_pos, idx, -jnp.inf)
        bits = pltpu.bitcast(idx, I32)
        ikey_ref[pl.ds(k0, ck), :] = bits ^ ((bits >> 31) & 0x7FFFFFFF)
        return carry

    lax.fori_loop(0, nkc, index_chunk, 0)

    def count(pred):
        def body(c, cnt):
            k0 = pl.multiple_of(c * ck, ck)
            v = ikey_ref[pl.ds(k0, ck), :]
            s_pos = k0 + lax.broadcasted_iota(I32, (ck, Q_BLOCK), 0)
            ind = pred(v, s_pos)
            return cnt + jnp.sum(ind.reshape(ck // 8, 8, Q_BLOCK), axis=0)
        cnt8 = lax.fori_loop(0, nkc, body, jnp.zeros((8, Q_BLOCK), I32))
        return jnp.sum(cnt8, axis=0, keepdims=True)

    def fill_chunk(c, carry):
        ikey_ref[pl.ds(pl.multiple_of(c * ck, ck), ck), :] = jnp.full((ck, Q_BLOCK), KEY_NEG_INF, I32)
        return carry
    lax.fori_loop(nkc, seq // ck, fill_chunk, 0)

    grp = seq // 32
    assert grp % 8 == 0

    def bit_transpose(v, carry):
        off = pl.multiple_of(v * 8, 8)
        a = [ikey_ref[pl.ds(grp * j + off, 8), :] for j in range(32)]
        j, msk = 16, 0x0000FFFF
        while j:
            k = 0
            while k < 32:
                t = (a[k] ^ lax.shift_right_logical(a[k + j], np.int32(j))) & np.int32(msk)
                a[k] = a[k] ^ t
                a[k + j] = a[k + j] ^ (t << np.int32(j))
                k = (k + j + 1) & ~j
            j >>= 1
            msk = (msk ^ (msk << j)) & 0xFFFFFFFF if j else msk
        a[0] = ~a[0]
        for i in range(32):
            planes_ref[pl.ds(grp * i + off, 8), :] = a[i]
        return carry
    lax.fori_loop(0, grp // 8, bit_transpose, 0)

    alive = jnp.full((grp, Q_BLOCK), -1, I32)
    n_gt = jnp.zeros((1, Q_BLOCK), I32)
    thr_u = jnp.zeros((1, Q_BLOCK), I32)
    for i in range(32):
        x = alive & planes_ref[grp * i:grp * (i + 1), :]
        c1 = jnp.sum(lax.population_count(x), axis=0, keepdims=True)
        take = (n_gt + c1) >= topk
        alive = jnp.where(take, x, alive ^ x)
        n_gt = jnp.where(take, n_gt, n_gt + c1)
        thr_u = jnp.where(take, thr_u | np.int32(-2 ** 31 if i == 0 else 1 << (31 - i)), thr_u)
    thr = thr_u ^ np.int32(INT_MIN)

    need = topk - n_gt
    n_ge = n_gt + jnp.sum(lax.population_count(alive), axis=0, keepdims=True)
    xp_ref[...] = jnp.full((8, Q_BLOCK), seq, I32)

    @pl.when(jnp.max(n_ge) > topk)
    def _():
        x = jnp.zeros((1, Q_BLOCK), I32)
        for bit in range(int(np.log2(seq)) - 1, -1, -1):
            cand = x + np.int32(1 << bit)
            hc = count(lambda v, p, cand=cand: jnp.where(v == thr, jnp.where(p < cand, 1, 0), 0))
            x = jnp.where(hc < need, cand, x)
        xp_ref[...] = jnp.broadcast_to(x + 1, (8, Q_BLOCK))

    xp = xp_ref[0:1, :]

    def bias_chunk(c, carry):
        k0 = pl.multiple_of(c * ck, ck)
        v = ikey_ref[pl.ds(k0, ck), :]
        s_pos = k0 + lax.broadcasted_iota(I32, (ck, Q_BLOCK), 0)
        tie = jnp.where(s_pos < xp, 0.0, NEG)
        b = jnp.where(v > thr, 0.0, jnp.where(v == thr, tie, NEG))
        b = jnp.where(s_pos <= t_pos, b, NEG).astype(F32)
        bias_ref[:, pl.ds(k0, ck)] = b.T
        return carry

    lax.fori_loop(0, nkc, bias_chunk, 0)

    zpad = jnp.zeros((Q_BLOCK, 2 * KV_RANK - KV_RANK - D_ROPE), BF16)
    for h in range(N_HEADS_A):
        qall_ref[h * Q_BLOCK:(h + 1) * Q_BLOCK, :] = jnp.concatenate(
            [qlat_ref[:, h * KV_RANK:(h + 1) * KV_RANK], qrope_ref[:, h * D_ROPE:(h + 1) * D_ROPE], zpad], axis=1)
    rows = N_HEADS_A * Q_BLOCK
    hg = DSA_HEAD_GROUP
    grows = hg * Q_BLOCK
    groups = [slice(g * grows, (g + 1) * grows) for g in range(N_HEADS_A // hg)]
    ck = DSA_ATT_CK
    n_att = (q0 + Q_BLOCK + ck - 1) // ck

    def scores(rs, k0):
        s = _dot(qall_ref[rs, :], kcatT_ref[:, pl.ds(k0, ck)])
        return (s.reshape(hg, Q_BLOCK, ck) + bias_ref[:, pl.ds(k0, ck)][None]).reshape(grows, ck)

    for rs in groups:
        s = scores(rs, 0)
        m0 = jnp.broadcast_to(jnp.max(s, axis=1, keepdims=True), (grows, LANES))
        pbuf_ref[0, rs, :] = jnp.exp2(s - pltpu.repeat(m0, ck // LANES, axis=1)).astype(BF16)
        m_ref[rs, :] = m0
    acc_ref[...] = jnp.zeros((rows, 2 * KV_RANK), F32)

    def attn_chunk(c, carry):
        k0 = pl.multiple_of(c * ck, ck)
        cv_prev = ckv_ref[pl.ds(k0 - ck, ck), :]
        slot = c & 1
        olds = [(m_ref[rs, :], acc_ref[rs, :]) for rs in groups]
        news = []
        for rs, (m_old, acc_old) in zip(groups, olds):
            s = scores(rs, k0)
            pv_prev = _dot(pbuf_ref[1 - slot, rs, :], cv_prev)
            m_new = jnp.maximum(m_old, jnp.max(s, axis=1, keepdims=True))
            alpha = jnp.exp2(m_old - m_new)
            p = jnp.exp2(s - pltpu.repeat(m_new, ck // LANES, axis=1)).astype(BF16)
            news.append((m_new, p, pltpu.repeat(alpha, 2, axis=1) * (acc_old + pv_prev)))
        for rs, (m_new, p, acc_new) in zip(groups, news):
            m_ref[rs, :] = m_new
            pbuf_ref[slot, rs, :] = p
            acc_ref[rs, :] = acc_new
        return carry

    lax.fori_loop(1, n_att, attn_chunk, 0)

    last = n_att - 1
    cv_last = ckv_ref[pl.ds(pl.multiple_of(last * ck, ck), ck), :]
    for rs in groups:
        acc_ref[rs, :] = acc_ref[rs, :] + _dot(pbuf_ref[last & 1, rs, :], cv_last)

    o = (acc_ref[:, :KV_RANK] / acc_ref[:, KV_RANK:]).astype(BF16)
    o_lat = jnp.concatenate([o[h * Q_BLOCK:(h + 1) * Q_BLOCK, :] for h in range(N_HEADS_A)], axis=1)
    out_ref[...] = _dot(o_lat, wuv_ref[...]).astype(BF16)


def _dsa(qlat, qrope, qidx_h, weffT, kidx, kcatT, ckv, wuv_bd, seq):
    b = qlat.shape[0]
    nb = seq // Q_BLOCK
    topk = min(TOPK_MAX, seq // 4)
    blkmap = lambda i, j: (i, j, 0)
    seqmap = lambda i, j: (i, 0, 0)
    rows = N_HEADS_A * Q_BLOCK
    return pl.pallas_call(
        functools.partial(_dsa_kernel, seq=seq, topk=topk),
        grid=(b, nb),
        in_specs=[pl.BlockSpec((None, Q_BLOCK, N_HEADS_A * KV_RANK), blkmap),
                  pl.BlockSpec((None, Q_BLOCK, N_HEADS_A * D_ROPE), blkmap),
                  pl.BlockSpec((None, IDX_HEADS, Q_BLOCK, IDX_DIM), lambda i, j: (i, 0, j, 0)),
                  pl.BlockSpec((None, IDX_HEADS, Q_BLOCK), lambda i, j: (i, 0, j)),
                  pl.BlockSpec((None, seq, IDX_DIM), seqmap),
                  pl.BlockSpec((None, 2 * KV_RANK, seq), seqmap),
                  pl.BlockSpec((None, seq, 2 * KV_RANK), seqmap),
                  pl.BlockSpec((N_HEADS_A * KV_RANK, N_HEADS_A * V_DIM), lambda i, j: (0, 0))],
        out_specs=pl.BlockSpec((None, Q_BLOCK, N_HEADS_A * V_DIM), blkmap),
        out_shape=jax.ShapeDtypeStruct((b, seq, N_HEADS_A * V_DIM), BF16),
        scratch_shapes=[pltpu.VMEM((seq, Q_BLOCK), I32),
                        pltpu.VMEM((seq, Q_BLOCK), I32),
                        pltpu.VMEM((Q_BLOCK, seq), F32),
                        pltpu.VMEM((rows, 2 * KV_RANK), BF16),
                        pltpu.VMEM((8, Q_BLOCK), I32),
                        pltpu.VMEM((rows, LANES), F32),
                        pltpu.VMEM((rows, 2 * KV_RANK), F32),
                        pltpu.VMEM((2, rows, DSA_ATT_CK), BF16)],
        compiler_params=pltpu.CompilerParams(dimension_semantics=("arbitrary", "arbitrary"),
                                             vmem_limit_bytes=VMEM_LIMIT),
        name="dsa",
    )(qlat, qrope, qidx_h, weffT, kidx, kcatT, ckv, wuv_bd)


DILATED_PATTERNS = ((128, 1), (512, 4), (2048, 16))
DIL_GROUP = 4


def _dilated_kernel(q_ref, k_ref, v_ref, out_ref, acc_ref, m_ref, l_ref, acc4_ref, m4_ref, l4_ref,
                    q4_ref, k4_ref, v4_ref, q16_ref, k16_ref, v16_ref, band_ref, tri_ref, *, seq):
    qb = Q_BLOCK
    lane = lax.broadcasted_iota(I32, (1, LANES), 1)
    head0 = lane < HEAD_DIM
    scale = HEAD_DIM ** -0.5

    a2 = lax.broadcasted_iota(I32, (qb, 2 * qb), 0)
    c2 = lax.broadcasted_iota(I32, (qb, 2 * qb), 1)
    band_ref[...] = jnp.where(c2 >= a2, jnp.where(c2 <= a2 + qb, 0.0, NEG), NEG)
    a1 = lax.broadcasted_iota(I32, (qb, qb), 0)
    c1 = lax.broadcasted_iota(I32, (qb, qb), 1)
    tri_ref[...] = jnp.where(c1 <= a1, 0.0, NEG)

    def rows(ref, start, stride=1):
        if stride == 1:
            return ref[pl.ds(start, qb), :]
        return ref[pl.ds(start, qb, stride=stride), :]

    def put(ref, start, stride, val):
        if stride == 1:
            ref[pl.ds(start, qb), :] = val
        else:
            ref[pl.ds(start, qb, stride=stride), :] = val

    n4, n16 = seq // 4, seq // 16
    for src, m4, m16 in ((q_ref, q4_ref, q16_ref), (k_ref, k4_ref, k16_ref), (v_ref, v4_ref, v16_ref)):
        def to_m4(c, carry, src=src, m4=m4):
            for r4 in range(4):
                m4[pl.ds(r4 * n4 + c * qb, qb), :] = rows(src, c * (4 * qb) + r4, 4)
            return carry
        lax.fori_loop(0, n4 // qb, to_m4, 0)

        def to_m16(r4, carry, m4=m4, m16=m16):
            for j in range(4):
                m16[pl.ds((r4 + 4 * j) * n16, qb), :] = rows(m4, r4 * n4 + j, 4)
            return carry
        lax.fori_loop(0, 4, to_m16, 0)

    def load_kv(kv_refs, start):
        kk = rows(kv_refs[0], start)
        return (jnp.where(head0, kk, 0.0).astype(BF16), jnp.where(head0, 0.0, kk).astype(BF16),
                rows(kv_refs[1], start).astype(BF16))

    def cat(xs):
        return xs[0] if len(xs) == 1 else jnp.concatenate(xs, axis=0)

    def attend_all(qs, kv_lists):
        scores = [[_dot_nt(q, cat([kv[hh] for kv in kvs])) for hh in range(2)] for q, kvs in zip(qs, kv_lists)]
        probs = []
        for ss, kvs in zip(scores, kv_lists):
            bias = tri_ref[...] if len(kvs) == 1 else band_ref[...]
            row = []
            for s in ss:
                s = s + bias
                m = jnp.max(s, axis=1, keepdims=True)
                p = jnp.exp(s - m)
                row.append((p.astype(BF16), m, jnp.sum(p, axis=1, keepdims=True)))
            probs.append(row)
        outs = []
        for row, kvs in zip(probs, kv_lists):
            vv = cat([kv[2] for kv in kvs])
            pvs = [_dot(p, vv) for p, _, _ in row]
            outs.append((jnp.where(head0, pvs[0], pvs[1]), jnp.where(head0, row[0][1], row[1][1]),
                         jnp.where(head0, row[0][2], row[1][2])))
        return outs

    def merged(old, new):
        mn = jnp.maximum(old[1], new[1])
        a_old = jnp.exp(old[1] - mn)
        a_new = jnp.exp(new[1] - mn)
        return old[0] * a_old + new[0] * a_new, mn, old[2] * a_old + new[2] * a_new

    def run_group(srcs, blocks, probs, stats_in, store):
        kvs = [load_kv(srcs[1:], s) for s in blocks]
        qs = [(rows(srcs[0], q_start) * scale).astype(BF16) for q_start, _ in probs]
        olds = [None if stats_in is None else tuple(rows(r, q_start) for r in stats_in) for q_start, _ in probs]
        news = attend_all(qs, [[kvs[b] for b in kb] for _, kb in probs])
        for i, (new, old) in enumerate(zip(news, olds)):
            store(i, new if old is None else merged(old, new))

    def chain_probs(blocks, key_only_first):
        return [(blocks[u], [u - 1, u] if u > 0 else [u]) for u in range(1 if key_only_first else 0, len(blocks))]

    assert DILATED_PATTERNS == ((qb, 1), (4 * qb, 4), (16 * qb, 16)) and n16 == qb
    g = DIL_GROUP
    stats4 = (acc4_ref, m4_ref, l4_ref)
    stats = (acc_ref, m_ref, l_ref)

    def store_to(refs, starts):
        def store(i, vals):
            for ref, val in zip(refs, vals):
                put(ref, starts[i], 1, val)
        return store

    def p16_body(j, carry):
        blocks = [(r4 + 4 * j) * n16 for r4 in range(4)]

        def store(r4, vals):
            for ref, val in zip(stats4, vals):
                put(ref, r4 * n4 + j, 4, val)
        run_group((q16_ref, k16_ref, v16_ref), blocks, [(blocks[r4], [r4]) for r4 in range(4)], None, store)
        return carry
    lax.fori_loop(0, 4, p16_body, 0)

    def p4_body(r4, carry):
        base = pl.multiple_of(r4 * n4, n4)
        blocks = [base + u * qb for u in range(n4 // qb)]
        run_group((q4_ref, k4_ref, v4_ref), blocks, chain_probs(blocks, False), stats4, store_to(stats4, blocks))
        return carry
    lax.fori_loop(0, 4, p4_body, 0)

    def to_nat(c, carry):
        for r4 in range(4):
            for s4, s1 in zip(stats4, stats):
                put(s1, c * (4 * qb) + r4, 4, rows(s4, r4 * n4 + c * qb))
        return carry
    lax.fori_loop(0, n4 // qb, to_nat, 0)

    nb = seq // qb
    assert nb % g == 0
    nat = (q_ref, k_ref, v_ref)
    blocks0 = [u * qb for u in range(g)]
    run_group(nat, blocks0, chain_probs(blocks0, False), stats, store_to(stats, blocks0))

    def p1_body(i, carry):
        base = pl.multiple_of(i * (g * qb), g * qb)
        blocks = [base + (u - 1) * qb for u in range(g + 1)]
        run_group(nat, blocks, chain_probs(blocks, True), stats, store_to(stats, blocks[1:]))
        return carry
    lax.fori_loop(1, nb // g, p1_body, 0)

    out_ref[...] = (acc_ref[...] / l_ref[...]).astype(BF16)


def _dilated(qb, kb, vb, seq):
    b = qb.shape[0]
    npair = N_HEADS_B * HEAD_DIM // LANES
    spec = pl.BlockSpec((None, seq, LANES), lambda i, j: (i, 0, j))
    return pl.pallas_call(
        functools.partial(_dilated_kernel, seq=seq),
        grid=(b, npair),
        in_specs=[spec, spec, spec],
        out_specs=spec,
        out_shape=jax.ShapeDtypeStruct((b, seq, N_HEADS_B * HEAD_DIM), BF16),
        scratch_shapes=[pltpu.VMEM((seq, LANES), F32)] * 12 + [pltpu.VMEM((Q_BLOCK, 2 * Q_BLOCK), F32),
                                                              pltpu.VMEM((Q_BLOCK, Q_BLOCK), F32)],
        compiler_params=pltpu.CompilerParams(dimension_semantics=("arbitrary", "arbitrary"),
                                             vmem_limit_bytes=VMEM_LIMIT),
        name="dilated",
    )(qb, kb, vb)


def _memkv_kernel(mem_ref, g_ref, w_ref, k_ref, v_ref):
    m = _rms(mem_ref[...], g_ref[...]).astype(BF16)
    kv = _dot(m, w_ref[...])
    k_ref[...] = kv[:, :D_MODEL].astype(BF16)
    v_ref[...] = kv[:, D_MODEL:].astype(BF16)


def _memkv(mem2, g, w_kv):
    n = mem2.shape[0]
    tm = MEM_TOKENS
    row = lambda i: (i, 0)
    const = lambda i: (0, 0)
    return pl.pallas_call(
        _memkv_kernel,
        grid=(n // tm,),
        in_specs=[pl.BlockSpec((tm, D_MODEL), row), pl.BlockSpec((1, D_MODEL), const),
                  pl.BlockSpec((D_MODEL, 2 * D_MODEL), const)],
        out_specs=[pl.BlockSpec((tm, D_MODEL), row)] * 2,
        out_shape=[jax.ShapeDtypeStruct((n, D_MODEL), BF16)] * 2,
        compiler_params=pltpu.CompilerParams(dimension_semantics=("arbitrary",), vmem_limit_bytes=VMEM_LIMIT),
        name="memkv",
    )(mem2, g, w_kv)


def _cross_kernel(x_ref, oa_ref, ob_ref, wout_ref, g_ref, wq_ref, kc_ref, vc_ref, wo_ref, out_ref):
    half = N_HEADS_A * V_DIM
    x1 = x_ref[...] + _dot(oa_ref[...], wout_ref[:half, :]) + _dot(ob_ref[...], wout_ref[half:, :])
    qc = _dot(_rms(x1, g_ref[...]).astype(BF16), wq_ref[...]).astype(BF16)
    scale = CROSS_HEAD_DIM ** -0.5
    ocs = []
    for h in range(CROSS_HEADS):
        sl = slice(h * CROSS_HEAD_DIM, (h + 1) * CROSS_HEAD_DIM)
        s = _dot_nt(qc[:, sl], kc_ref[:, sl]) * scale
        p = jnp.exp(s - jnp.max(s, axis=1, keepdims=True))
        p = p / jnp.sum(p, axis=1, keepdims=True)
        ocs.append(_dot(p.astype(BF16), vc_ref[:, sl]).astype(BF16))
    oc = jnp.concatenate(ocs, axis=1)
    out_ref[...] = x1 + _dot(oc, wo_ref[...])


def _cross(x2, oa, ob, w_out, g, w_q, kc, vc, w_o, seq, tm):
    n = x2.shape[0]
    per_seq = seq // tm
    row = lambda i: (i, 0)
    const = lambda i: (0, 0)
    memmap = lambda i: (i // per_seq, 0)
    half = N_HEADS_A * V_DIM
    return pl.pallas_call(
        _cross_kernel,
        grid=(n // tm,),
        in_specs=[pl.BlockSpec((tm, D_MODEL), row), pl.BlockSpec((tm, half), row), pl.BlockSpec((tm, half), row),
                  pl.BlockSpec((D_MODEL, D_MODEL), const), pl.BlockSpec((1, D_MODEL), const),
                  pl.BlockSpec((D_MODEL, D_MODEL), const),
                  pl.BlockSpec((MEM_TOKENS, D_MODEL), memmap), pl.BlockSpec((MEM_TOKENS, D_MODEL), memmap),
                  pl.BlockSpec((D_MODEL, D_MODEL), const)],
        out_specs=pl.BlockSpec((tm, D_MODEL), row),
        out_shape=jax.ShapeDtypeStruct((n, D_MODEL), F32),
        compiler_params=pltpu.CompilerParams(dimension_semantics=("arbitrary",), vmem_limit_bytes=VMEM_LIMIT),
        name="cross",
    )(x2, oa, ob, w_out, g, w_q, kc, vc, w_o)


MLP_FF_CHUNK = 1024


def _mlp_kernel(x_ref, g_ref, wup_ref, wdown_ref, gf_ref, out_ref):
    x = x_ref[...]
    hm = _rms(x, g_ref[...]).astype(BF16)
    y = x
    for c in range(D_FF // MLP_FF_CHUNK):
        sl = slice(c * MLP_FF_CHUNK, (c + 1) * MLP_FF_CHUNK)
        u = jnp.maximum(_dot(hm, wup_ref[:, sl]), 0.0)
        y = y + _dot((u * u).astype(BF16), wdown_ref[sl, :])
    out_ref[...] = _rms(y, gf_ref[...])


def _mlp(x2, g, w_up, w_down, gf, tm):
    n = x2.shape[0]
    row = lambda i: (i, 0)
    const = lambda i: (0, 0)
    return pl.pallas_call(
        _mlp_kernel,
        grid=(n // tm,),
        in_specs=[pl.BlockSpec((tm, D_MODEL), row), pl.BlockSpec((1, D_MODEL), const),
                  pl.BlockSpec((D_MODEL, D_FF), const), pl.BlockSpec((D_FF, D_MODEL), const),
                  pl.BlockSpec((1, D_MODEL), const)],
        out_specs=pl.BlockSpec((tm, D_MODEL), row),
        out_shape=jax.ShapeDtypeStruct((n, D_MODEL), F32),
        compiler_params=pltpu.CompilerParams(dimension_semantics=("arbitrary",), vmem_limit_bytes=VMEM_LIMIT),
        name="mlp",
    )(x2, g, w_up, w_down, gf)


def _block_diag(w):
    h, a, b = w.shape
    eye = jnp.eye(h, dtype=w.dtype)
    return (eye[:, None, :, None] * w[:, :, None, :]).reshape(h * a, h * b)


def kernel(x, mem, norm_mix_g, w_in, kv_norm_g, w_uk, w_uv, w_out, norm_cross_g, norm_mem_g,
           w_q_cross, w_kv_cross, w_o_cross, norm_mlp_g, w_up, w_down, norm_final_g):
    b, seq, _ = x.shape
    assert seq == 2048 and w_in.shape[0] == 1, "kernel is specialised to SEQ=2048, DEPTH=1"
    tm = 512

    wi = w_in[0]
    col = lambda k: wi[:, _OFF[k]:_OFF[k + 1]]
    misc = jnp.concatenate([col(3), col(5), col(6), jnp.zeros((D_MODEL, LANES - 104), F32)], axis=1)
    w_cat = jnp.concatenate([col(0), col(1), col(2), misc, col(4), col(7)], axis=1).astype(BF16)
    wuk_bd = _block_diag(w_uk[0]).astype(BF16)
    wuv_bd = _block_diag(w_uv[0]).astype(BF16)
    tabs = _rope_tables(seq)

    x2 = x.reshape(b * seq, D_MODEL)
    qlat, qrope, ckv, miscp, qidx, qb, kb, vb = _inproj(
        x2, norm_mix_g[0][None], w_cat, wuk_bd, kv_norm_g[0][None], tabs, seq, tm)

    miscp = miscp.reshape(b, seq, LANES)
    krope = miscp[:, :, MISC_KR:MISC_KR + D_ROPE].astype(BF16)
    kidx = miscp[:, :, MISC_KI:MISC_KI + IDX_DIM].astype(BF16)
    weffT = jnp.swapaxes(miscp[:, :, MISC_WI:MISC_WI + IDX_HEADS], 1, 2)
    ckv = ckv.reshape(b, seq, KV_RANK)
    kcat = jnp.concatenate([ckv, krope, jnp.zeros((b, seq, 2 * KV_RANK - KV_RANK - D_ROPE), BF16)], axis=2)
    kcatT = jnp.swapaxes(kcat, 1, 2)
    qidx_h = jnp.swapaxes(qidx.reshape(b, seq, IDX_HEADS, IDX_DIM), 1, 2)

    ckv1 = jnp.concatenate([ckv, jnp.ones((b, seq, KV_RANK), BF16)], axis=2)
    o_a = _dsa(qlat.reshape(b, seq, -1), qrope.reshape(b, seq, -1), qidx_h, weffT, kidx, kcatT, ckv1, wuv_bd, seq)
    o_b = _dilated(qb.reshape(b, seq, -1), kb.reshape(b, seq, -1), vb.reshape(b, seq, -1), seq)

    kc, vc = _memkv(mem.reshape(b * MEM_TOKENS, D_MODEL), norm_mem_g[0][None], w_kv_cross[0].astype(BF16))
    xc = _cross(x2, o_a.reshape(b * seq, -1), o_b.reshape(b * seq, -1), w_out[0].astype(BF16),
                norm_cross_g[0][None], w_q_cross[0].astype(BF16), kc, vc, w_o_cross[0].astype(BF16), seq, tm)
    out = _mlp(xc, norm_mlp_g[0][None], w_up[0].astype(BF16), w_down[0].astype(BF16), norm_final_g[None], tm)
    return out.reshape(b, seq, D_MODEL)
```

```python
import functools

import numpy as np
import jax
import jax.numpy as jnp
from jax import lax
from jax.experimental import pallas as pl
from jax.experimental.pallas import tpu as pltpu

F32 = jnp.float32
BF16 = jnp.bfloat16
I32 = jnp.int32

D_MODEL = 1024
HEAD_DIM = 64
N_HEADS_A = 8
N_HEADS_B = 8
D_NOPE = 64
D_ROPE = 32
KV_RANK = 128
V_DIM = 64
IDX_HEADS = 8
IDX_DIM = 64
TOPK_MAX = 256
Q_BLOCK = 128
CROSS_HEADS = 4
CROSS_HEAD_DIM = 256
MEM_TOKENS = 256
D_FF = 4096
ROPE_THETA = 10000.0
NORM_EPS = 1e-6

LANES = 128
VMEM_LIMIT = 48 * 1024 * 1024
NEG = -1e30
INT_MIN = -2 ** 31
KEY_NEG_INF = -2139095041
DSA_Q_SCALE = float((D_NOPE + D_ROPE) ** -0.5 * np.log2(np.e))

_OFF = np.cumsum([0, 512, 256, 128, 32, 512, 64, 8, 1536])
C_QN, C_QR, C_CKV, C_MISC, C_QI, C_QB, C_KB, C_VB, C_END = 0, 512, 768, 896, 1024, 1536, 2048, 2560, 3072
MISC_KI, MISC_KR, MISC_WI = 0, 64, 96


def _dot(a, b):
    return jnp.dot(a, b, preferred_element_type=F32)


def _dot_nt(a, b):
    return lax.dot_general(a, b, (((1,), (1,)), ((), ())), preferred_element_type=F32)


def _rms(x, g):
    return x * lax.rsqrt(jnp.mean(x * x, axis=-1, keepdims=True) + NORM_EPS) * g


def _rope_tables(seq):
    pos = jnp.arange(seq, dtype=F32)[:, None]
    lane = np.arange(LANES)

    def tables(d, lanes_local, active):
        half = d // 2
        inv = ROPE_THETA ** (-jnp.arange(0, d, 2, dtype=F32) / d)
        ang = pos * inv[None, :]
        cos, sin = jnp.cos(ang), jnp.sin(ang)
        f = (lanes_local % d) % half
        first = jnp.asarray(((lanes_local % d) < half) & active)[None, :]
        second = jnp.asarray(((lanes_local % d) >= half) & active)[None, :]
        act = jnp.asarray(active)[None, :]
        c = jnp.where(act, cos[:, f], 0.0)
        sa = jnp.where(first, -sin[:, f], 0.0)
        sb = jnp.where(second, sin[:, f], 0.0)
        return c, sa, sb

    all_on = np.ones(LANES, bool)
    c64, sa64, sb64 = tables(64, lane, all_on)
    c32, sa32, sb32 = tables(32, lane, all_on)
    ki_on = lane < MISC_KR
    kr_on = (lane >= MISC_KR) & (lane < MISC_WI)
    ckr, sa16m, sb16m = tables(32, lane - MISC_KR, kr_on)
    cki, sa32m, sb32m = tables(64, lane, ki_on)
    w_scale = (IDX_HEADS ** -0.5) * (IDX_DIM ** -0.5)
    wi_on = jnp.asarray((lane >= MISC_WI) & (lane < MISC_WI + IDX_HEADS))[None, :]
    cosm = ckr + cki + jnp.where(wi_on, w_scale, 0.0)
    return jnp.stack([c64, sa64, sb64, c32, sa32, sb32, cosm, sa16m, sb16m, sa32m, sb32m], axis=0)


def _rope_lanes(x, cos, sa, sb, half):
    outs = []
    for c in range(x.shape[1] // LANES):
        xs = x[:, c * LANES:(c + 1) * LANES]
        outs.append(xs * cos + pltpu.roll(xs, LANES - half, 1) * sa + pltpu.roll(xs, half, 1) * sb)
    return outs[0] if len(outs) == 1 else jnp.concatenate(outs, axis=1)


def _inproj_kernel(x_ref, g_ref, w_ref, wuk_ref, kvg_ref, tab_ref,
                   qlat_ref, qrope_ref, kcat_ref, ckv1_ref, misc_ref, qidx_ref, qb_ref, kb_ref, vb_ref):
    h = _rms(x_ref[...], g_ref[...]).astype(BF16)

    def proj(c0, c1):
        return _dot(h, w_ref[:, c0:c1])

    c64, sa64, sb64 = tab_ref[0], tab_ref[1], tab_ref[2]
    c32, sa32, sb32 = tab_ref[3], tab_ref[4], tab_ref[5]

    qn = proj(C_QN, C_QR).astype(BF16)
    qlat_ref[...] = (_dot(qn, wuk_ref[...]) * DSA_Q_SCALE).astype(BF16)
    qrope_ref[...] = (_rope_lanes(proj(C_QR, C_CKV), c32, sa32, sb32, D_ROPE // 2) * DSA_Q_SCALE).astype(BF16)
    ckv = _rms(proj(C_CKV, C_MISC), kvg_ref[...]).astype(BF16)
    pm = proj(C_MISC, C_QI)
    misc = (pm * tab_ref[6]
            + pltpu.roll(pm, LANES - 16, 1) * tab_ref[7] + pltpu.roll(pm, 16, 1) * tab_ref[8]
            + pltpu.roll(pm, LANES - 32, 1) * tab_ref[9] + pltpu.roll(pm, 32, 1) * tab_ref[10])
    misc_ref[...] = misc
    kcat_ref[...] = jnp.concatenate([ckv, misc.astype(BF16)], axis=1)
    ckv1_ref[...] = jnp.concatenate([ckv, jnp.ones(ckv.shape, BF16)], axis=1)
    qidx_ref[...] = _rope_lanes(proj(C_QI, C_QB), c64, sa64, sb64, IDX_DIM // 2).astype(BF16)
    qb_ref[...] = _rope_lanes(proj(C_QB, C_KB), c64, sa64, sb64, HEAD_DIM // 2)
    kb_ref[...] = _rope_lanes(proj(C_KB, C_VB), c64, sa64, sb64, HEAD_DIM // 2)
    vb_ref[...] = proj(C_VB, C_END)


def _inproj(x2, g, w_cat, wuk_bd, kvg, tabs, seq, tm):
    n = x2.shape[0]
    per_seq = seq // tm
    row = lambda i: (i, 0)
    const = lambda i: (0, 0)
    outs = [(D_MODEL, BF16), (N_HEADS_A * D_ROPE, BF16), (2 * KV_RANK, BF16), (2 * KV_RANK, BF16), (LANES, F32),
            (IDX_HEADS * IDX_DIM, BF16), (512, F32), (512, F32), (512, F32)]
    return pl.pallas_call(
        _inproj_kernel,
        grid=(n // tm,),
        in_specs=[pl.BlockSpec((tm, D_MODEL), row),
                  pl.BlockSpec((1, D_MODEL), const),
                  pl.BlockSpec((D_MODEL, C_END), const),
                  pl.BlockSpec((N_HEADS_A * D_NOPE, N_HEADS_A * KV_RANK), const),
                  pl.BlockSpec((1, KV_RANK), const),
                  pl.BlockSpec((11, tm, LANES), lambda i: (0, i % per_seq, 0))],
        out_specs=[pl.BlockSpec((tm, w), row) for w, _ in outs],
        out_shape=[jax.ShapeDtypeStruct((n, w), dt) for w, dt in outs],
        compiler_params=pltpu.CompilerParams(dimension_semantics=("arbitrary",), vmem_limit_bytes=VMEM_LIMIT),
        name="inproj",
    )(x2, g, w_cat, wuk_bd, kvg, tabs)


DSA_CK = 512
DSA_ATT_CK = 512
DSA_HEAD_GROUP = 1


def _dsa_kernel(qlat_ref, qrope_ref, qidx_ref, weff_ref, kcat_ref, ckv_ref, wuv_ref,
                out_ref, ikey_ref, planes_ref, bias_ref, qall_ref, xp_ref, m_ref, acc_ref, pbuf_ref, *, seq, topk):
    blk = pl.program_id(1)
    q0 = blk * Q_BLOCK
    nkc = (q0 + Q_BLOCK + DSA_CK - 1) // DSA_CK
    ck = DSA_CK

    qi = qidx_ref[...]
    qi_all = jnp.concatenate([qi[:, h * IDX_DIM:(h + 1) * IDX_DIM] for h in range(IDX_HEADS)], axis=0)
    ki0 = KV_RANK + MISC_KI
    t_pos = q0 + lax.broadcasted_iota(I32, (ck, Q_BLOCK), 1)

    def index_chunk(c, carry):
        k0 = pl.multiple_of(c * ck, ck)
        kc = kcat_ref[pl.ds(k0, ck), ki0:ki0 + IDX_DIM]
        idx = jnp.zeros((ck, Q_BLOCK), F32)
        for g in range(IDX_HEADS // 2):
            lg = _dot_nt(kc, qi_all[2 * g * Q_BLOCK:(2 * g + 2) * Q_BLOCK, :])
            for hh in range(2):
                h = 2 * g + hh
                idx = idx + jnp.maximum(lg[:, hh * Q_BLOCK:(hh + 1) * Q_BLOCK], 0.0) * weff_ref[h:h + 1, :]
        s_pos = k0 + lax.broadcasted_iota(I32, (ck, Q_BLOCK), 0)
        idx = jnp.where(idx == 0.0, 0.0, idx)
        idx = jnp.where(s_pos <= t_pos, idx, -jnp.inf)
        bits = pltpu.bitcast(idx, I32)
        ikey_ref[pl.ds(k0, ck), :] = bits ^ ((bits >> 31) & 0x7FFFFFFF)
        return carry

    lax.fori_loop(0, nkc, index_chunk, 0)

    def count(pred):
        def body(c, cnt):
            k0 = pl.multiple_of(c * ck, ck)
            v = ikey_ref[pl.ds(k0, ck), :]
            s_pos = k0 + lax.broadcasted_iota(I32, (ck, Q_BLOCK), 0)
            ind = pred(v, s_pos)
            return cnt + jnp.sum(ind.reshape(ck // 8, 8, Q_BLOCK), axis=0)
        cnt8 = lax.fori_loop(0, nkc, body, jnp.zeros((8, Q_BLOCK), I32))
        return jnp.sum(cnt8, axis=0, keepdims=True)

    def fill_chunk(c, carry):
        ikey_ref[pl.ds(pl.multiple_of(c * ck, ck), ck), :] = jnp.full((ck, Q_BLOCK), KEY_NEG_INF, I32)
        return carry
    lax.fori_loop(nkc, seq // ck, fill_chunk, 0)

    grp = seq // 32
    assert grp % 8 == 0

    def bit_transpose(v, carry):
        off = pl.multiple_of(v * 8, 8)
        a = [ikey_ref[pl.ds(grp * j + off, 8), :] for j in range(32)]
        j, msk = 16, 0x0000FFFF
        while j:
            k = 0
            while k < 32:
                t = (a[k] ^ lax.shift_right_logical(a[k + j], np.int32(j))) & np.int32(msk)
                a[k] = a[k] ^ t
                a[k + j] = a[k + j] ^ (t << np.int32(j))
                k = (k + j + 1) & ~j
            j >>= 1
            msk = (msk ^ (msk << j)) & 0xFFFFFFFF if j else msk
        a[0] = ~a[0]
        for i in range(32):
            planes_ref[pl.ds(grp * i + off, 8), :] = a[i]
        return carry
    lax.fori_loop(0, grp // 8, bit_transpose, 0)

    alive = jnp.full((grp, Q_BLOCK), -1, I32)
    n_gt = jnp.zeros((1, Q_BLOCK), I32)
    thr_u = jnp.zeros((1, Q_BLOCK), I32)
    for i in range(32):
        x = alive & planes_ref[grp * i:grp * (i + 1), :]
        c1 = jnp.sum(lax.population_count(x), axis=0, keepdims=True)
        take = (n_gt + c1) >= topk
        alive = jnp.where(take, x, alive ^ x)
        n_gt = jnp.where(take, n_gt, n_gt + c1)
        thr_u = jnp.where(take, thr_u | np.int32(-2 ** 31 if i == 0 else 1 << (31 - i)), thr_u)
    thr = thr_u ^ np.int32(INT_MIN)

    need = topk - n_gt
    n_ge = n_gt + jnp.sum(lax.population_count(alive), axis=0, keepdims=True)
    xp_ref[...] = jnp.full((8, Q_BLOCK), seq, I32)

    @pl.when(jnp.max(n_ge) > topk)
    def _():
        x = jnp.zeros((1, Q_BLOCK), I32)
        for bit in range(int(np.log2(seq)) - 1, -1, -1):
            cand = x + np.int32(1 << bit)
            hc = count(lambda v, p, cand=cand: jnp.where(v == thr, jnp.where(p < cand, 1, 0), 0))
            x = jnp.where(hc < need, cand, x)
        xp_ref[...] = jnp.broadcast_to(x + 1, (8, Q_BLOCK))

    xp = xp_ref[0:1, :]

    def bias_chunk(c, carry):
        k0 = pl.multiple_of(c * ck, ck)
        v = ikey_ref[pl.ds(k0, ck), :]
        s_pos = k0 + lax.broadcasted_iota(I32, (ck, Q_BLOCK), 0)
        tie = jnp.where(s_pos < xp, 0.0, NEG)
        b = jnp.where(v > thr, 0.0, jnp.where(v == thr, tie, NEG))
        b = jnp.where(s_pos <= t_pos, b, NEG).astype(F32)
        bias_ref[:, pl.ds(k0, ck)] = b.T
        return carry

    lax.fori_loop(0, nkc, bias_chunk, 0)

    zlo = jnp.zeros((Q_BLOCK, MISC_KR), BF16)
    zhi = jnp.zeros((Q_BLOCK, KV_RANK - MISC_KR - D_ROPE), BF16)
    for h in range(N_HEADS_A):
        qall_ref[h * Q_BLOCK:(h + 1) * Q_BLOCK, :] = jnp.concatenate(
            [qlat_ref[:, h * KV_RANK:(h + 1) * KV_RANK], zlo, qrope_ref[:, h * D_ROPE:(h + 1) * D_ROPE], zhi], axis=1)
    rows = N_HEADS_A * Q_BLOCK
    hg = DSA_HEAD_GROUP
    grows = hg * Q_BLOCK
    groups = [slice(g * grows, (g + 1) * grows) for g in range(N_HEADS_A // hg)]
    ck = DSA_ATT_CK
    n_att = (q0 + Q_BLOCK + ck - 1) // ck

    def scores(rs, k0):
        s = _dot_nt(qall_ref[rs, :], kcat_ref[pl.ds(k0, ck), :])
        return (s.reshape(hg, Q_BLOCK, ck) + bias_ref[:, pl.ds(k0, ck)][None]).reshape(grows, ck)

    for rs in groups:
        s = scores(rs, 0)
        m0 = jnp.broadcast_to(jnp.max(s, axis=1, keepdims=True), (grows, LANES))
        pbuf_ref[0, rs, :] = jnp.exp2(s - pltpu.repeat(m0, ck // LANES, axis=1)).astype(BF16)
        m_ref[rs, :] = m0
    acc_ref[...] = jnp.zeros((rows, 2 * KV_RANK), F32)

    def attn_chunk(c, carry):
        k0 = pl.multiple_of(c * ck, ck)
        cv_prev = ckv_ref[pl.ds(k0 - ck, ck), :]
        slot = c & 1
        olds = [(m_ref[rs, :], acc_ref[rs, :]) for rs in groups]
        news = []
        for rs, (m_old, acc_old) in zip(groups, olds):
            s = scores(rs, k0)
            pv_prev = _dot(pbuf_ref[1 - slot, rs, :], cv_prev)
            m_new = jnp.maximum(m_old, jnp.max(s, axis=1, keepdims=True))
            alpha = jnp.exp2(m_old - m_new)
            p = jnp.exp2(s - pltpu.repeat(m_new, ck // LANES, axis=1)).astype(BF16)
            news.append((m_new, p, pltpu.repeat(alpha, 2, axis=1) * (acc_old + pv_prev)))
        for rs, (m_new, p, acc_new) in zip(groups, news):
            m_ref[rs, :] = m_new
            pbuf_ref[slot, rs, :] = p
            acc_ref[rs, :] = acc_new
        return carry

    lax.fori_loop(1, n_att, attn_chunk, 0)

    last = n_att - 1
    cv_last = ckv_ref[pl.ds(pl.multiple_of(last * ck, ck), ck), :]
    for rs in groups:
        acc_ref[rs, :] = acc_ref[rs, :] + _dot(pbuf_ref[last & 1, rs, :], cv_last)

    o = (acc_ref[:, :KV_RANK] / acc_ref[:, KV_RANK:]).astype(BF16)
    o_lat = jnp.concatenate([o[h * Q_BLOCK:(h + 1) * Q_BLOCK, :] for h in range(N_HEADS_A)], axis=1)
    out_ref[...] = _dot(o_lat, wuv_ref[...]).astype(BF16)


def _dsa(qlat, qrope, qidx, weffT, kcat, ckv1, wuv_bd, seq):
    b = qlat.shape[0]
    nb = seq // Q_BLOCK
    topk = min(TOPK_MAX, seq // 4)
    blkmap = lambda i, j: (i, j, 0)
    seqmap = lambda i, j: (i, 0, 0)
    rows = N_HEADS_A * Q_BLOCK
    return pl.pallas_call(
        functools.partial(_dsa_kernel, seq=seq, topk=topk),
        grid=(b, nb),
        in_specs=[pl.BlockSpec((None, Q_BLOCK, N_HEADS_A * KV_RANK), blkmap),
                  pl.BlockSpec((None, Q_BLOCK, N_HEADS_A * D_ROPE), blkmap),
                  pl.BlockSpec((None, Q_BLOCK, IDX_HEADS * IDX_DIM), blkmap),
                  pl.BlockSpec((None, IDX_HEADS, Q_BLOCK), lambda i, j: (i, 0, j)),
                  pl.BlockSpec((None, seq, 2 * KV_RANK), seqmap),
                  pl.BlockSpec((None, seq, 2 * KV_RANK), seqmap),
                  pl.BlockSpec((N_HEADS_A * KV_RANK, N_HEADS_A * V_DIM), lambda i, j: (0, 0))],
        out_specs=pl.BlockSpec((None, Q_BLOCK, N_HEADS_A * V_DIM), blkmap),
        out_shape=jax.ShapeDtypeStruct((b, seq, N_HEADS_A * V_DIM), BF16),
        scratch_shapes=[pltpu.VMEM((seq, Q_BLOCK), I32),
                        pltpu.VMEM((seq, Q_BLOCK), I32),
                        pltpu.VMEM((Q_BLOCK, seq), F32),
                        pltpu.VMEM((rows, 2 * KV_RANK), BF16),
                        pltpu.VMEM((8, Q_BLOCK), I32),
                        pltpu.VMEM((rows, LANES), F32),
                        pltpu.VMEM((rows, 2 * KV_RANK), F32),
                        pltpu.VMEM((2, rows, DSA_ATT_CK), BF16)],
        compiler_params=pltpu.CompilerParams(dimension_semantics=("arbitrary", "arbitrary"),
                                             vmem_limit_bytes=VMEM_LIMIT),
        name="dsa",
    )(qlat, qrope, qidx, weffT, kcat, ckv1, wuv_bd)


DILATED_PATTERNS = ((128, 1), (512, 4), (2048, 16))
DIL_GROUP = 8


def _dilated_kernel(q_ref, k_ref, v_ref, out_ref, acc_ref, m_ref, l_ref, acc4_ref, m4_ref, l4_ref,
                    q4_ref, k4_ref, v4_ref, q16_ref, k16_ref, v16_ref, band_ref, tri_ref, *, seq):
    qb = Q_BLOCK
    lane = lax.broadcasted_iota(I32, (1, LANES), 1)
    head0 = lane < HEAD_DIM
    scale = HEAD_DIM ** -0.5

    a2 = lax.broadcasted_iota(I32, (qb, 2 * qb), 0)
    c2 = lax.broadcasted_iota(I32, (qb, 2 * qb), 1)
    band_ref[...] = jnp.where(c2 >= a2, jnp.where(c2 <= a2 + qb, 0.0, NEG), NEG)
    a1 = lax.broadcasted_iota(I32, (qb, qb), 0)
    c1 = lax.broadcasted_iota(I32, (qb, qb), 1)
    tri_ref[...] = jnp.where(c1 <= a1, 0.0, NEG)

    def rows(ref, start, stride=1):
        if stride == 1:
            return ref[pl.ds(start, qb), :]
        return ref[pl.ds(start, qb, stride=stride), :]

    def put(ref, start, stride, val):
        if stride == 1:
            ref[pl.ds(start, qb), :] = val
        else:
            ref[pl.ds(start, qb, stride=stride), :] = val

    n4, n16 = seq // 4, seq // 16
    for src, m4, m16 in ((q_ref, q4_ref, q16_ref), (k_ref, k4_ref, k16_ref), (v_ref, v4_ref, v16_ref)):
        def to_m4(c, carry, src=src, m4=m4):
            for r4 in range(4):
                m4[pl.ds(r4 * n4 + c * qb, qb), :] = rows(src, c * (4 * qb) + r4, 4)
            return carry
        lax.fori_loop(0, n4 // qb, to_m4, 0)

        def to_m16(r4, carry, m4=m4, m16=m16):
            for j in range(4):
                m16[pl.ds((r4 + 4 * j) * n16, qb), :] = rows(m4, r4 * n4 + j, 4)
            return carry
        lax.fori_loop(0, 4, to_m16, 0)

    def load_kv(kv_refs, start):
        kk = rows(kv_refs[0], start)
        v1 = jnp.concatenate([rows(kv_refs[1], start).astype(BF16), jnp.ones((qb, LANES), BF16)], axis=1)
        return jnp.where(head0, kk, 0.0).astype(BF16), jnp.where(head0, 0.0, kk).astype(BF16), v1

    def cat(xs):
        return xs[0] if len(xs) == 1 else jnp.concatenate(xs, axis=0)

    def attend_all(qs, kv_lists):
        scores = [[_dot_nt(q, cat([kv[hh] for kv in kvs])) for hh in range(2)] for q, kvs in zip(qs, kv_lists)]
        probs = []
        for ss, kvs in zip(scores, kv_lists):
            bias = tri_ref[...] if len(kvs) == 1 else band_ref[...]
            row = []
            for s in ss:
                s = s + bias
                m = jnp.broadcast_to(jnp.max(s, axis=1, keepdims=True), (qb, LANES))
                p = jnp.exp(s - pltpu.repeat(m, s.shape[1] // LANES, axis=1))
                row.append((p.astype(BF16), m))
            probs.append(row)
        outs = []
        for row, kvs in zip(probs, kv_lists):
            vv = cat([kv[2] for kv in kvs])
            pvs = [_dot(p, vv) for p, _ in row]
            outs.append((jnp.where(head0, pvs[0][:, :LANES], pvs[1][:, :LANES]),
                         jnp.where(head0, row[0][1], row[1][1]),
                         jnp.where(head0, pvs[0][:, LANES:], pvs[1][:, LANES:])))
        return outs

    def merged(old, new):
        mn = jnp.maximum(old[1], new[1])
        a_old = jnp.exp(old[1] - mn)
        a_new = jnp.exp(new[1] - mn)
        return old[0] * a_old + new[0] * a_new, mn, old[2] * a_old + new[2] * a_new

    def run_group(srcs, blocks, probs, stats_in, store):
        kvs = [load_kv(srcs[1:], s) for s in blocks]
        qs = [(rows(srcs[0], q_start) * scale).astype(BF16) for q_start, _ in probs]
        olds = [None if stats_in is None else tuple(rows(r, q_start) for r in stats_in) for q_start, _ in probs]
        news = attend_all(qs, [[kvs[b] for b in kb] for _, kb in probs])
        for i, (new, old) in enumerate(zip(news, olds)):
            store(i, new if old is None else merged(old, new))

    def chain_probs(blocks, key_only_first, off=0):
        return [(blocks[u], [off + u - 1, off + u] if u > 0 else [off + u])
                for u in range(1 if key_only_first else 0, len(blocks))]

    assert DILATED_PATTERNS == ((qb, 1), (4 * qb, 4), (16 * qb, 16)) and n16 == qb
    g = DIL_GROUP
    stats4 = (acc4_ref, m4_ref, l4_ref)
    stats = (acc_ref, m_ref, l_ref)

    def store_to(refs, starts):
        def store(i, vals):
            for ref, val in zip(refs, vals):
                put(ref, starts[i], 1, val)
        return store

    per16 = g // 4

    def p16_body(jj, carry):
        rj = [(r4, jj * per16 + dj) for dj in range(per16) for r4 in range(4)]
        blocks = [(r4 + 4 * j) * n16 for r4, j in rj]

        def store(i, vals):
            r4, j = rj[i]
            for ref, val in zip(stats4, vals):
                put(ref, r4 * n4 + j, 4, val)
        run_group((q16_ref, k16_ref, v16_ref), blocks, [(blocks[i], [i]) for i in range(len(rj))], None, store)
        return carry
    lax.fori_loop(0, 4 // per16, p16_body, 0)

    nb4 = n4 // qb
    per4 = g // nb4

    def p4_body(rr, carry):
        blocks, probs = [], []
        for dr in range(per4):
            base = pl.multiple_of((rr * per4 + dr) * n4, n4)
            chain = [base + u * qb for u in range(nb4)]
            probs += chain_probs(chain, False, len(blocks))
            blocks += chain
        run_group((q4_ref, k4_ref, v4_ref), blocks, probs, stats4, store_to(stats4, blocks))
        return carry
    lax.fori_loop(0, 4 // per4, p4_body, 0)

    def to_nat(c, carry):
        for r4 in range(4):
            for s4, s1 in zip(stats4, stats):
                put(s1, c * (4 * qb) + r4, 4, rows(s4, r4 * n4 + c * qb))
        return carry
    lax.fori_loop(0, n4 // qb, to_nat, 0)

    nb = seq // qb
    assert nb % g == 0
    nat = (q_ref, k_ref, v_ref)
    blocks0 = [u * qb for u in range(g)]
    run_group(nat, blocks0, chain_probs(blocks0, False), stats, store_to(stats, blocks0))

    def p1_body(i, carry):
        base = pl.multiple_of(i * (g * qb), g * qb)
        blocks = [base + (u - 1) * qb for u in range(g + 1)]
        run_group(nat, blocks, chain_probs(blocks, True), stats, store_to(stats, blocks[1:]))
        return carry
    lax.fori_loop(1, nb // g, p1_body, 0)

    out_ref[...] = (acc_ref[...] / l_ref[...]).astype(BF16)


def _dilated(qb, kb, vb, seq):
    b = qb.shape[0]
    npair = N_HEADS_B * HEAD_DIM // LANES
    spec = pl.BlockSpec((None, seq, LANES), lambda i, j: (i, 0, j))
    return pl.pallas_call(
        functools.partial(_dilated_kernel, seq=seq),
        grid=(b, npair),
        in_specs=[spec, spec, spec],
        out_specs=spec,
        out_shape=jax.ShapeDtypeStruct((b, seq, N_HEADS_B * HEAD_DIM), BF16),
        scratch_shapes=[pltpu.VMEM((seq, LANES), F32)] * 12 + [pltpu.VMEM((Q_BLOCK, 2 * Q_BLOCK), F32),
                                                              pltpu.VMEM((Q_BLOCK, Q_BLOCK), F32)],
        compiler_params=pltpu.CompilerParams(dimension_semantics=("arbitrary", "arbitrary"),
                                             vmem_limit_bytes=VMEM_LIMIT),
        name="dilated",
    )(qb, kb, vb)


def _memkv_kernel(mem_ref, g_ref, w_ref, k_ref, v_ref):
    m = _rms(mem_ref[...], g_ref[...]).astype(BF16)
    kv = _dot(m, w_ref[...])
    k_ref[...] = kv[:, :D_MODEL].astype(BF16)
    v_ref[...] = kv[:, D_MODEL:].astype(BF16)


def _memkv(mem2, g, w_kv):
    n = mem2.shape[0]
    tm = MEM_TOKENS
    row = lambda i: (i, 0)
    const = lambda i: (0, 0)
    return pl.pallas_call(
        _memkv_kernel,
        grid=(n // tm,),
        in_specs=[pl.BlockSpec((tm, D_MODEL), row), pl.BlockSpec((1, D_MODEL), const),
                  pl.BlockSpec((D_MODEL, 2 * D_MODEL), const)],
        out_specs=[pl.BlockSpec((tm, D_MODEL), row)] * 2,
        out_shape=[jax.ShapeDtypeStruct((n, D_MODEL), BF16)] * 2,
        compiler_params=pltpu.CompilerParams(dimension_semantics=("arbitrary",), vmem_limit_bytes=VMEM_LIMIT),
        name="memkv",
    )(mem2, g, w_kv)


def _cross_kernel(x_ref, oa_ref, ob_ref, wout_ref, g_ref, wq_ref, kc_ref, vc_ref, wo_ref, out_ref):
    half = N_HEADS_A * V_DIM
    x1 = x_ref[...] + _dot(oa_ref[...], wout_ref[:half, :]) + _dot(ob_ref[...], wout_ref[half:, :])
    qc = _dot(_rms(x1, g_ref[...]).astype(BF16), wq_ref[...]).astype(BF16)
    scale = CROSS_HEAD_DIM ** -0.5
    ocs = []
    for h in range(CROSS_HEADS):
        sl = slice(h * CROSS_HEAD_DIM, (h + 1) * CROSS_HEAD_DIM)
        s = _dot_nt(qc[:, sl], kc_ref[:, sl]) * scale
        p = jnp.exp(s - jnp.max(s, axis=1, keepdims=True))
        p = p / jnp.sum(p, axis=1, keepdims=True)
        ocs.append(_dot(p.astype(BF16), vc_ref[:, sl]).astype(BF16))
    oc = jnp.concatenate(ocs, axis=1)
    out_ref[...] = x1 + _dot(oc, wo_ref[...])


def _cross(x2, oa, ob, w_out, g, w_q, kc, vc, w_o, seq, tm):
    n = x2.shape[0]
    per_seq = seq // tm
    row = lambda i: (i, 0)
    const = lambda i: (0, 0)
    memmap = lambda i: (i // per_seq, 0)
    half = N_HEADS_A * V_DIM
    return pl.pallas_call(
        _cross_kernel,
        grid=(n // tm,),
        in_specs=[pl.BlockSpec((tm, D_MODEL), row), pl.BlockSpec((tm, half), row), pl.BlockSpec((tm, half), row),
                  pl.BlockSpec((D_MODEL, D_MODEL), const), pl.BlockSpec((1, D_MODEL), const),
                  pl.BlockSpec((D_MODEL, D_MODEL), const),
                  pl.BlockSpec((MEM_TOKENS, D_MODEL), memmap), pl.BlockSpec((MEM_TOKENS, D_MODEL), memmap),
                  pl.BlockSpec((D_MODEL, D_MODEL), const)],
        out_specs=pl.BlockSpec((tm, D_MODEL), row),
        out_shape=jax.ShapeDtypeStruct((n, D_MODEL), F32),
        compiler_params=pltpu.CompilerParams(dimension_semantics=("arbitrary",), vmem_limit_bytes=VMEM_LIMIT),
        name="cross",
    )(x2, oa, ob, w_out, g, w_q, kc, vc, w_o)


MLP_FF_CHUNK = 1024


def _mlp_kernel(x_ref, g_ref, wup_ref, wdown_ref, gf_ref, out_ref):
    x = x_ref[...]
    hm = _rms(x, g_ref[...]).astype(BF16)
    y = x
    for c in range(D_FF // MLP_FF_CHUNK):
        sl = slice(c * MLP_FF_CHUNK, (c + 1) * MLP_FF_CHUNK)
        u = jnp.maximum(_dot(hm, wup_ref[:, sl]), 0.0)
        y = y + _dot((u * u).astype(BF16), wdown_ref[sl, :])
    out_ref[...] = _rms(y, gf_ref[...])


def _mlp(x2, g, w_up, w_down, gf, tm):
    n = x2.shape[0]
    row = lambda i: (i, 0)
    const = lambda i: (0, 0)
    return pl.pallas_call(
        _mlp_kernel,
        grid=(n // tm,),
        in_specs=[pl.BlockSpec((tm, D_MODEL), row), pl.BlockSpec((1, D_MODEL), const),
                  pl.BlockSpec((D_MODEL, D_FF), const), pl.BlockSpec((D_FF, D_MODEL), const),
                  pl.BlockSpec((1, D_MODEL), const)],
        out_specs=pl.BlockSpec((tm, D_MODEL), row),
        out_shape=jax.ShapeDtypeStruct((n, D_MODEL), F32),
        compiler_params=pltpu.CompilerParams(dimension_semantics=("arbitrary",), vmem_limit_bytes=VMEM_LIMIT),
        name="mlp",
    )(x2, g, w_up, w_down, gf)


def _block_diag(w):
    h, a, b = w.shape
    eye = jnp.eye(h, dtype=w.dtype)
    return (eye[:, None, :, None] * w[:, :, None, :]).reshape(h * a, h * b)


def kernel(x, mem, norm_mix_g, w_in, kv_norm_g, w_uk, w_uv, w_out, norm_cross_g, norm_mem_g,
           w_q_cross, w_kv_cross, w_o_cross, norm_mlp_g, w_up, w_down, norm_final_g):
    b, seq, _ = x.shape
    assert seq == 2048 and w_in.shape[0] == 1, "kernel is specialised to SEQ=2048, DEPTH=1"
    tm = 512

    wi = w_in[0]
    col = lambda k: wi[:, _OFF[k]:_OFF[k + 1]]
    misc = jnp.concatenate([col(5), col(3), col(6), jnp.zeros((D_MODEL, LANES - 104), F32)], axis=1)
    w_cat = jnp.concatenate([col(0), col(1), col(2), misc, col(4), col(7)], axis=1).astype(BF16)
    wuk_bd = _block_diag(w_uk[0]).astype(BF16)
    wuv_bd = _block_diag(w_uv[0]).astype(BF16)
    tabs = _rope_tables(seq)

    x2 = x.reshape(b * seq, D_MODEL)
    qlat, qrope, kcat, ckv1, miscp, qidx, qb, kb, vb = _inproj(
        x2, norm_mix_g[0][None], w_cat, wuk_bd, kv_norm_g[0][None], tabs, seq, tm)

    weffT = jnp.swapaxes(miscp.reshape(b, seq, LANES)[:, :, MISC_WI:MISC_WI + IDX_HEADS], 1, 2)
    o_a = _dsa(qlat.reshape(b, seq, -1), qrope.reshape(b, seq, -1), qidx.reshape(b, seq, -1), weffT,
               kcat.reshape(b, seq, -1), ckv1.reshape(b, seq, -1), wuv_bd, seq)
    o_b = _dilated(qb.reshape(b, seq, -1), kb.reshape(b, seq, -1), vb.reshape(b, seq, -1), seq)

    kc, vc = _memkv(mem.reshape(b * MEM_TOKENS, D_MODEL), norm_mem_g[0][None], w_kv_cross[0].astype(BF16))
    xc = _cross(x2, o_a.reshape(b * seq, -1), o_b.reshape(b * seq, -1), w_out[0].astype(BF16),
                norm_cross_g[0][None], w_q_cross[0].astype(BF16), kc, vc, w_o_cross[0].astype(BF16), seq, tm)
    out = _mlp(xc, norm_mlp_g[0][None], w_up[0].astype(BF16), w_down[0].astype(BF16), norm_final_g[None], tm)
    return out.reshape(b, seq, D_MODEL)
```

```python
import functools

import numpy as np
import jax
import jax.numpy as jnp
from jax import lax
from jax.experimental import pallas as pl
from jax.experimental.pallas import tpu as pltpu

F32 = jnp.float32
BF16 = jnp.bfloat16
I32 = jnp.int32

D_MODEL = 1024
HEAD_DIM = 64
N_HEADS_A = 8
N_HEADS_B = 8
D_NOPE = 64
D_ROPE = 32
KV_RANK = 128
V_DIM = 64
IDX_HEADS = 8
IDX_DIM = 64
TOPK_MAX = 256
Q_BLOCK = 128
CROSS_HEADS = 4
CROSS_HEAD_DIM = 256
MEM_TOKENS = 256
D_FF = 4096
ROPE_THETA = 10000.0
NORM_EPS = 1e-6

LANES = 128
VMEM_LIMIT = 48 * 1024 * 1024
NEG = -1e30
INT_MIN = -2 ** 31
KEY_NEG_INF = -2139095041
DSA_Q_SCALE = float((D_NOPE + D_ROPE) ** -0.5 * np.log2(np.e))

_OFF = np.cumsum([0, 512, 256, 128, 32, 512, 64, 8, 1536])
C_QN, C_QR, C_CKV, C_MISC, C_QI, C_QB, C_KB, C_VB, C_END = 0, 512, 768, 896, 1024, 1536, 2048, 2560, 3072
MISC_KI, MISC_KR, MISC_WI = 0, 64, 96


def _dot(a, b):
    return jnp.dot(a, b, preferred_element_type=F32)


def _dot_nt(a, b):
    return lax.dot_general(a, b, (((1,), (1,)), ((), ())), preferred_element_type=F32)


def _rms(x, g):
    return x * lax.rsqrt(jnp.mean(x * x, axis=-1, keepdims=True) + NORM_EPS) * g


def _rope_tables(seq):
    pos = jnp.arange(seq, dtype=F32)[:, None]
    lane = np.arange(LANES)

    def tables(d, lanes_local, active):
        half = d // 2
        inv = ROPE_THETA ** (-jnp.arange(0, d, 2, dtype=F32) / d)
        ang = pos * inv[None, :]
        cos, sin = jnp.cos(ang), jnp.sin(ang)
        f = (lanes_local % d) % half
        first = jnp.asarray(((lanes_local % d) < half) & active)[None, :]
        second = jnp.asarray(((lanes_local % d) >= half) & active)[None, :]
        act = jnp.asarray(active)[None, :]
        c = jnp.where(act, cos[:, f], 0.0)
        sa = jnp.where(first, -sin[:, f], 0.0)
        sb = jnp.where(second, sin[:, f], 0.0)
        return c, sa, sb

    all_on = np.ones(LANES, bool)
    c64, sa64, sb64 = tables(64, lane, all_on)
    c32, sa32, sb32 = tables(32, lane, all_on)
    ki_on = lane < MISC_KR
    kr_on = (lane >= MISC_KR) & (lane < MISC_WI)
    ckr, sa16m, sb16m = tables(32, lane - MISC_KR, kr_on)
    cki, sa32m, sb32m = tables(64, lane, ki_on)
    w_scale = (IDX_HEADS ** -0.5) * (IDX_DIM ** -0.5)
    wi_on = jnp.asarray((lane >= MISC_WI) & (lane < MISC_WI + IDX_HEADS))[None, :]
    cosm = ckr + cki + jnp.where(wi_on, w_scale, 0.0)
    return jnp.stack([c64, sa64, sb64, c32, sa32, sb32, cosm, sa16m, sb16m, sa32m, sb32m], axis=0)


def _rope_lanes(x, cos, sa, sb, half):
    outs = []
    for c in range(x.shape[1] // LANES):
        xs = x[:, c * LANES:(c + 1) * LANES]
        outs.append(xs * cos + pltpu.roll(xs, LANES - half, 1) * sa + pltpu.roll(xs, half, 1) * sb)
    return outs[0] if len(outs) == 1 else jnp.concatenate(outs, axis=1)


def _inproj_kernel(x_ref, g_ref, w_ref, wuk_ref, kvg_ref, tab_ref,
                   qlat_ref, qrope_ref, kcat_ref, ckv1_ref, misc_ref, qidx_ref, qb_ref, kb_ref, vb_ref, kcatT_ref):
    h = _rms(x_ref[...], g_ref[...]).astype(BF16)

    def proj(c0, c1):
        return _dot(h, w_ref[:, c0:c1])

    c64, sa64, sb64 = tab_ref[0], tab_ref[1], tab_ref[2]
    c32, sa32, sb32 = tab_ref[3], tab_ref[4], tab_ref[5]

    qn = proj(C_QN, C_QR).astype(BF16)
    qlat_ref[...] = (_dot(qn, wuk_ref[...]) * DSA_Q_SCALE).astype(BF16)
    qrope_ref[...] = (_rope_lanes(proj(C_QR, C_CKV), c32, sa32, sb32, D_ROPE // 2) * DSA_Q_SCALE).astype(BF16)
    ckv32 = _rms(proj(C_CKV, C_MISC), kvg_ref[...])
    ckv = ckv32.astype(BF16)
    pm = proj(C_MISC, C_QI)
    misc = (pm * tab_ref[6]
            + pltpu.roll(pm, LANES - 16, 1) * tab_ref[7] + pltpu.roll(pm, 16, 1) * tab_ref[8]
            + pltpu.roll(pm, LANES - 32, 1) * tab_ref[9] + pltpu.roll(pm, 32, 1) * tab_ref[10])
    misc_ref[...] = misc
    kcat_ref[...] = jnp.concatenate([ckv, misc.astype(BF16)], axis=1)
    kcatT_ref[...] = jnp.concatenate([ckv32.T, misc.T], axis=0).astype(BF16)
    ckv1_ref[...] = jnp.concatenate([ckv, jnp.ones(ckv.shape, BF16)], axis=1)
    qidx_ref[...] = _rope_lanes(proj(C_QI, C_QB), c64, sa64, sb64, IDX_DIM // 2).astype(BF16)
    qb_ref[...] = _rope_lanes(proj(C_QB, C_KB), c64, sa64, sb64, HEAD_DIM // 2)
    kb_ref[...] = _rope_lanes(proj(C_KB, C_VB), c64, sa64, sb64, HEAD_DIM // 2)
    vb_ref[...] = proj(C_VB, C_END)


def _inproj(x2, g, w_cat, wuk_bd, kvg, tabs, seq, tm):
    n = x2.shape[0]
    per_seq = seq // tm
    row = lambda i: (i, 0)
    const = lambda i: (0, 0)
    outs = [(D_MODEL, BF16), (N_HEADS_A * D_ROPE, BF16), (2 * KV_RANK, BF16), (2 * KV_RANK, BF16), (LANES, F32),
            (IDX_HEADS * IDX_DIM, BF16), (512, F32), (512, F32), (512, F32)]
    return pl.pallas_call(
        _inproj_kernel,
        grid=(n // tm,),
        in_specs=[pl.BlockSpec((tm, D_MODEL), row),
                  pl.BlockSpec((1, D_MODEL), const),
                  pl.BlockSpec((D_MODEL, C_END), const),
                  pl.BlockSpec((N_HEADS_A * D_NOPE, N_HEADS_A * KV_RANK), const),
                  pl.BlockSpec((1, KV_RANK), const),
                  pl.BlockSpec((11, tm, LANES), lambda i: (0, i % per_seq, 0))],
        out_specs=[pl.BlockSpec((tm, w), row) for w, _ in outs]
        + [pl.BlockSpec((2 * KV_RANK, tm), lambda i: (i // per_seq, i % per_seq))],
        out_shape=[jax.ShapeDtypeStruct((n, w), dt) for w, dt in outs]
        + [jax.ShapeDtypeStruct((n // seq * 2 * KV_RANK, seq), BF16)],
        compiler_params=pltpu.CompilerParams(dimension_semantics=("arbitrary",), vmem_limit_bytes=VMEM_LIMIT),
        name="inproj",
    )(x2, g, w_cat, wuk_bd, kvg, tabs)


DSA_CK = 512
DSA_ATT_CK = 512
DSA_SELECT_SIZES = (512, 1024, 2048)
DSA_HEAD_GROUP = 1


def _dsa_index(qidx_ref, weff_ref, kcat_ref, ikey_ref, q0, nkc):
    ck = DSA_CK

    qi = qidx_ref[...]
    qi_all = jnp.concatenate([qi[:, h * IDX_DIM:(h + 1) * IDX_DIM] for h in range(IDX_HEADS)], axis=0)
    ki0 = MISC_KI
    t_pos = q0 + lax.broadcasted_iota(I32, (ck, Q_BLOCK), 1)

    def index_chunk(c, carry):
        k0 = pl.multiple_of(c * ck, ck)
        kc = kcat_ref[pl.ds(k0, ck), ki0:ki0 + IDX_DIM]
        idx = jnp.zeros((ck, Q_BLOCK), F32)
        for g in range(IDX_HEADS // 2):
            lg = _dot_nt(kc, qi_all[2 * g * Q_BLOCK:(2 * g + 2) * Q_BLOCK, :])
            for hh in range(2):
                h = 2 * g + hh
                idx = idx + jnp.maximum(lg[:, hh * Q_BLOCK:(hh + 1) * Q_BLOCK], 0.0) * weff_ref[h:h + 1, :]
        s_pos = k0 + lax.broadcasted_iota(I32, (ck, Q_BLOCK), 0)
        idx = jnp.where(idx == 0.0, 0.0, idx)
        idx = jnp.where(s_pos <= t_pos, idx, -jnp.inf)
        bits = pltpu.bitcast(idx, I32)
        ikey_ref[pl.ds(k0, ck), :] = bits ^ ((bits >> 31) & 0x7FFFFFFF)
        return carry

    lax.fori_loop(0, nkc, index_chunk, 0)


def _radix_select(ikey_ref, planes_ref, sel_ref, nkc, grp, topk):
    ck = DSA_CK
    assert grp % 8 == 0 and (32 * grp) % ck == 0

    def fill_chunk(c, carry):
        ikey_ref[pl.ds(pl.multiple_of(c * ck, ck), ck), :] = jnp.full((ck, Q_BLOCK), KEY_NEG_INF, I32)
        return carry
    lax.fori_loop(nkc, 32 * grp // ck, fill_chunk, 0)

    def bit_transpose(v, carry):
        off = pl.multiple_of(v * 8, 8)
        a = [ikey_ref[pl.ds(grp * j + off, 8), :] for j in range(32)]
        j, msk = 16, 0x0000FFFF
        while j:
            k = 0
            while k < 32:
                t = (a[k] ^ lax.shift_right_logical(a[k + j], np.int32(j))) & np.int32(msk)
                a[k] = a[k] ^ t
                a[k + j] = a[k + j] ^ (t << np.int32(j))
                k = (k + j + 1) & ~j
            j >>= 1
            msk = (msk ^ (msk << j)) & 0xFFFFFFFF if j else msk
        a[0] = ~a[0]
        for i in range(32):
            planes_ref[pl.ds(grp * i + off, 8), :] = a[i]
        return carry
    lax.fori_loop(0, grp // 8, bit_transpose, 0)

    def count_bits(x):
        return jnp.sum(lax.population_count(x), axis=0, keepdims=True)

    def as_i32(u):
        return np.int32(u - (1 << 32) if u >= (1 << 31) else u)

    alive = jnp.full((grp, Q_BLOCK), -1, I32)
    n_gt = jnp.zeros((1, Q_BLOCK), I32)
    thr_u = jnp.zeros((1, Q_BLOCK), I32)
    for i in range(0, 32, 2):
        p_hi = planes_ref[grp * i:grp * (i + 1), :]
        p_lo = planes_ref[grp * (i + 1):grp * (i + 2), :]
        x1 = alive & p_hi
        x0 = alive ^ x1
        x11 = x1 & p_lo
        x10 = x1 ^ x11
        x01 = x0 & p_lo
        x00 = x0 ^ x01
        g3 = n_gt + count_bits(x11)
        g2 = g3 + count_bits(x10)
        g1 = g2 + count_bits(x01)
        t3, t2, t1 = g3 >= topk, g2 >= topk, g1 >= topk
        alive = jnp.where(t3, x11, jnp.where(t2, x10, jnp.where(t1, x01, x00)))
        n_gt = jnp.where(t3, n_gt, jnp.where(t2, g3, jnp.where(t1, g2, g1)))
        sh = 30 - i
        thr_u = thr_u | jnp.where(t3, as_i32(3 << sh), jnp.where(t2, as_i32(2 << sh),
                                                                   jnp.where(t1, as_i32(1 << sh), np.int32(0))))
    sel_ref[0:1, :] = thr_u ^ np.int32(INT_MIN)
    sel_ref[1:2, :] = topk - n_gt
    sel_ref[2:3, :] = n_gt + count_bits(alive)


def _dsa_bias(ikey_ref, bias_ref, xp_ref, sel_ref, q0, nkc, seq, topk):
    ck = DSA_CK
    t_pos = q0 + lax.broadcasted_iota(I32, (ck, Q_BLOCK), 1)
    thr, need, n_ge = sel_ref[0:1, :], sel_ref[1:2, :], sel_ref[2:3, :]

    def count(pred):
        def body(c, cnt):
            k0 = pl.multiple_of(c * ck, ck)
            v = ikey_ref[pl.ds(k0, ck), :]
            s_pos = k0 + lax.broadcasted_iota(I32, (ck, Q_BLOCK), 0)
            ind = pred(v, s_pos)
            return cnt + jnp.sum(ind.reshape(ck // 8, 8, Q_BLOCK), axis=0)
        cnt8 = lax.fori_loop(0, nkc, body, jnp.zeros((8, Q_BLOCK), I32))
        return jnp.sum(cnt8, axis=0, keepdims=True)

    xp_ref[...] = jnp.full((8, Q_BLOCK), seq, I32)

    @pl.when(jnp.max(n_ge) > topk)
    def _():
        x = jnp.zeros((1, Q_BLOCK), I32)
        for bit in range(int(np.log2(seq)) - 1, -1, -1):
            cand = x + np.int32(1 << bit)
            hc = count(lambda v, p, cand=cand: jnp.where(v == thr, jnp.where(p < cand, 1, 0), 0))
            x = jnp.where(hc < need, cand, x)
        xp_ref[...] = jnp.broadcast_to(x + 1, (8, Q_BLOCK))

    xp = xp_ref[0:1, :]

    def bias_chunk(c, carry):
        k0 = pl.multiple_of(c * ck, ck)
        v = ikey_ref[pl.ds(k0, ck), :]
        s_pos = k0 + lax.broadcasted_iota(I32, (ck, Q_BLOCK), 0)
        tie = jnp.where(s_pos < xp, 0.0, NEG)
        b = jnp.where(v > thr, 0.0, jnp.where(v == thr, tie, NEG))
        b = jnp.where(s_pos <= t_pos, b, NEG).astype(F32)
        bias_ref[:, pl.ds(k0, ck)] = b.T
        return carry

    lax.fori_loop(0, nkc, bias_chunk, 0)


def _dsa_kernel(qlat_ref, qrope_ref, qidx_ref, weff_ref, kcat_ref, kcatT_ref, ckv_ref, wuv_ref,
                out_ref, ikey_ref, planes_ref, bias_ref, qall_ref, xp_ref, sel_ref, m_ref, acc_ref, pbuf_ref,
                *, seq, topk):
    blk = pl.program_id(1)
    q0 = blk * Q_BLOCK
    nkc = (q0 + Q_BLOCK + DSA_CK - 1) // DSA_CK
    few_keys = q0 + Q_BLOCK <= topk

    @pl.when(jnp.logical_not(few_keys))
    def _():
        _dsa_index(qidx_ref, weff_ref, kcat_ref, ikey_ref, q0, nkc)
        kmax = q0 + Q_BLOCK
        sizes = [n for n in DSA_SELECT_SIZES if n <= seq]
        for lo, n in zip([0] + sizes[:-1], sizes):
            @pl.when(jnp.logical_and(kmax > lo, kmax <= n))
            def _(n=n):
                _radix_select(ikey_ref, planes_ref, sel_ref, nkc, n // 32, topk)
        _dsa_bias(ikey_ref, bias_ref, xp_ref, sel_ref, q0, nkc, seq, topk)

    @pl.when(few_keys)
    def _():
        assert topk <= DSA_ATT_CK
        t_pos = q0 + lax.broadcasted_iota(I32, (Q_BLOCK, DSA_ATT_CK), 0)
        s_pos = lax.broadcasted_iota(I32, (Q_BLOCK, DSA_ATT_CK), 1)
        bias_ref[:, 0:DSA_ATT_CK] = jnp.where(s_pos <= t_pos, 0.0, NEG)

    zlo = jnp.zeros((Q_BLOCK, MISC_KR), BF16)
    zhi = jnp.zeros((Q_BLOCK, KV_RANK - MISC_KR - D_ROPE), BF16)
    for h in range(N_HEADS_A):
        qall_ref[h * Q_BLOCK:(h + 1) * Q_BLOCK, :] = jnp.concatenate(
            [qlat_ref[:, h * KV_RANK:(h + 1) * KV_RANK], zlo, qrope_ref[:, h * D_ROPE:(h + 1) * D_ROPE], zhi], axis=1)
    rows = N_HEADS_A * Q_BLOCK
    hg = DSA_HEAD_GROUP
    grows = hg * Q_BLOCK
    groups = [slice(g * grows, (g + 1) * grows) for g in range(N_HEADS_A // hg)]
    ck = DSA_ATT_CK
    n_att = (q0 + Q_BLOCK + ck - 1) // ck

    def scores(rs, k0):
        s = _dot(qall_ref[rs, :], kcatT_ref[:, pl.ds(k0, ck)])
        return (s.reshape(hg, Q_BLOCK, ck) + bias_ref[:, pl.ds(k0, ck)][None]).reshape(grows, ck)

    for rs in groups:
        s = scores(rs, 0)
        m0 = jnp.broadcast_to(jnp.max(s, axis=1, keepdims=True), (grows, LANES))
        pbuf_ref[0, rs, :] = jnp.exp2(s - pltpu.repeat(m0, ck // LANES, axis=1)).astype(BF16)
        m_ref[rs, :] = m0
    acc_ref[...] = jnp.zeros((rows, 2 * KV_RANK), F32)

    def attn_chunk(c, carry):
        k0 = pl.multiple_of(c * ck, ck)
        cv_prev = ckv_ref[pl.ds(k0 - ck, ck), :]
        slot = c & 1
        olds = [(m_ref[rs, :], acc_ref[rs, :]) for rs in groups]
        news = []
        for rs, (m_old, acc_old) in zip(groups, olds):
            s = scores(rs, k0)
            pv_prev = _dot(pbuf_ref[1 - slot, rs, :], cv_prev)
            m_new = jnp.maximum(m_old, jnp.max(s, axis=1, keepdims=True))
            alpha = jnp.exp2(m_old - m_new)
            p = jnp.exp2(s - pltpu.repeat(m_new, ck // LANES, axis=1)).astype(BF16)
            news.append((m_new, p, pltpu.repeat(alpha, 2, axis=1) * (acc_old + pv_prev)))
        for rs, (m_new, p, acc_new) in zip(groups, news):
            m_ref[rs, :] = m_new
            pbuf_ref[slot, rs, :] = p
            acc_ref[rs, :] = acc_new
        return carry

    lax.fori_loop(1, n_att, attn_chunk, 0)

    last = n_att - 1
    cv_last = ckv_ref[pl.ds(pl.multiple_of(last * ck, ck), ck), :]
    for rs in groups:
        acc_ref[rs, :] = acc_ref[rs, :] + _dot(pbuf_ref[last & 1, rs, :], cv_last)

    o = (acc_ref[:, :KV_RANK] / acc_ref[:, KV_RANK:]).astype(BF16)
    o_lat = jnp.concatenate([o[h * Q_BLOCK:(h + 1) * Q_BLOCK, :] for h in range(N_HEADS_A)], axis=1)
    out_ref[...] = _dot(o_lat, wuv_ref[...]).astype(BF16)


def _dsa(qlat, qrope, qidx, weffT, kcat, kcatT, ckv1, wuv_bd, seq):
    b = qlat.shape[0]
    nb = seq // Q_BLOCK
    topk = min(TOPK_MAX, seq // 4)
    blkmap = lambda i, j: (i, j, 0)
    seqmap = lambda i, j: (i, 0, 0)
    rows = N_HEADS_A * Q_BLOCK
    return pl.pallas_call(
        functools.partial(_dsa_kernel, seq=seq, topk=topk),
        grid=(b, nb),
        in_specs=[pl.BlockSpec((None, Q_BLOCK, N_HEADS_A * KV_RANK), blkmap),
                  pl.BlockSpec((None, Q_BLOCK, N_HEADS_A * D_ROPE), blkmap),
                  pl.BlockSpec((None, Q_BLOCK, IDX_HEADS * IDX_DIM), blkmap),
                  pl.BlockSpec((None, IDX_HEADS, Q_BLOCK), lambda i, j: (i, 0, j)),
                  pl.BlockSpec((None, seq, LANES), lambda i, j: (i, 0, KV_RANK // LANES)),
                  pl.BlockSpec((None, 2 * KV_RANK, seq), seqmap),
                  pl.BlockSpec((None, seq, 2 * KV_RANK), seqmap),
                  pl.BlockSpec((N_HEADS_A * KV_RANK, N_HEADS_A * V_DIM), lambda i, j: (0, 0))],
        out_specs=pl.BlockSpec((None, Q_BLOCK, N_HEADS_A * V_DIM), blkmap),
        out_shape=jax.ShapeDtypeStruct((b, seq, N_HEADS_A * V_DIM), BF16),
        scratch_shapes=[pltpu.VMEM((seq, Q_BLOCK), I32),
                        pltpu.VMEM((seq, Q_BLOCK), I32),
                        pltpu.VMEM((Q_BLOCK, seq), F32),
                        pltpu.VMEM((rows, 2 * KV_RANK), BF16),
                        pltpu.VMEM((8, Q_BLOCK), I32),
                        pltpu.VMEM((8, Q_BLOCK), I32),
                        pltpu.VMEM((rows, LANES), F32),
                        pltpu.VMEM((rows, 2 * KV_RANK), F32),
                        pltpu.VMEM((2, rows, DSA_ATT_CK), BF16)],
        compiler_params=pltpu.CompilerParams(dimension_semantics=("arbitrary", "arbitrary"),
                                             vmem_limit_bytes=VMEM_LIMIT),
        name="dsa",
    )(qlat, qrope, qidx, weffT, kcat, kcatT, ckv1, wuv_bd)


DILATED_PATTERNS = ((128, 1), (512, 4), (2048, 16))
DIL_GROUP = 8


def _dilated_kernel(q_ref, k_ref, v_ref, out_ref, acc_ref, m_ref, l_ref, acc4_ref, m4_ref, l4_ref,
                    q4_ref, k4_ref, v4_ref, q16_ref, k16_ref, v16_ref, band_ref, tri_ref, *, seq):
    qb = Q_BLOCK
    lane = lax.broadcasted_iota(I32, (1, LANES), 1)
    head0 = lane < HEAD_DIM
    scale = HEAD_DIM ** -0.5

    a2 = lax.broadcasted_iota(I32, (qb, 2 * qb), 0)
    c2 = lax.broadcasted_iota(I32, (qb, 2 * qb), 1)
    band_ref[...] = jnp.where(c2 >= a2, jnp.where(c2 <= a2 + qb, 0.0, NEG), NEG)
    a1 = lax.broadcasted_iota(I32, (qb, qb), 0)
    c1 = lax.broadcasted_iota(I32, (qb, qb), 1)
    tri_ref[...] = jnp.where(c1 <= a1, 0.0, NEG)

    def rows(ref, start, stride=1):
        if stride == 1:
            return ref[pl.ds(start, qb), :]
        return ref[pl.ds(start, qb, stride=stride), :]

    def put(ref, start, stride, val):
        if stride == 1:
            ref[pl.ds(start, qb), :] = val
        else:
            ref[pl.ds(start, qb, stride=stride), :] = val

    n4, n16 = seq // 4, seq // 16
    for src, m4, m16 in ((q_ref, q4_ref, q16_ref), (k_ref, k4_ref, k16_ref), (v_ref, v4_ref, v16_ref)):
        def to_m4(c, carry, src=src, m4=m4):
            for r4 in range(4):
                m4[pl.ds(r4 * n4 + c * qb, qb), :] = rows(src, c * (4 * qb) + r4, 4)
            return carry
        lax.fori_loop(0, n4 // qb, to_m4, 0)

        def to_m16(r4, carry, m4=m4, m16=m16):
            for j in range(4):
                m16[pl.ds((r4 + 4 * j) * n16, qb), :] = rows(m4, r4 * n4 + j, 4)
            return carry
        lax.fori_loop(0, 4, to_m16, 0)

    def load_kv(kv_refs, start):
        kk = rows(kv_refs[0], start)
        v1 = jnp.concatenate([rows(kv_refs[1], start).astype(BF16), jnp.ones((qb, LANES), BF16)], axis=1)
        return jnp.where(head0, kk, 0.0).astype(BF16), jnp.where(head0, 0.0, kk).astype(BF16), v1

    def cat(xs):
        return xs[0] if len(xs) == 1 else jnp.concatenate(xs, axis=0)

    def attend_all(qs, kv_lists):
        scores = [[_dot_nt(q, cat([kv[hh] for kv in kvs])) for hh in range(2)] for q, kvs in zip(qs, kv_lists)]
        probs = []
        for ss, kvs in zip(scores, kv_lists):
            bias = tri_ref[...] if len(kvs) == 1 else band_ref[...]
            row = []
            for s in ss:
                s = s + bias
                m = jnp.broadcast_to(jnp.max(s, axis=1, keepdims=True), (qb, LANES))
                p = jnp.exp(s - pltpu.repeat(m, s.shape[1] // LANES, axis=1))
                row.append((p.astype(BF16), m))
            probs.append(row)
        outs = []
        for row, kvs in zip(probs, kv_lists):
            vv = cat([kv[2] for kv in kvs])
            pvs = [_dot(p, vv) for p, _ in row]
            outs.append((jnp.where(head0, pvs[0][:, :LANES], pvs[1][:, :LANES]),
                         jnp.where(head0, row[0][1], row[1][1]),
                         jnp.where(head0, pvs[0][:, LANES:], pvs[1][:, LANES:])))
        return outs

    def merged(old, new):
        mn = jnp.maximum(old[1], new[1])
        a_old = jnp.exp(old[1] - mn)
        a_new = jnp.exp(new[1] - mn)
        return old[0] * a_old + new[0] * a_new, mn, old[2] * a_old + new[2] * a_new

    def run_group(srcs, blocks, probs, stats_in, store):
        kvs = [load_kv(srcs[1:], s) for s in blocks]
        qs = [(rows(srcs[0], q_start) * scale).astype(BF16) for q_start, _ in probs]
        olds = [None if stats_in is None else tuple(rows(r, q_start) for r in stats_in) for q_start, _ in probs]
        news = attend_all(qs, [[kvs[b] for b in kb] for _, kb in probs])
        for i, (new, old) in enumerate(zip(news, olds)):
            store(i, new if old is None else merged(old, new))

    def chain_probs(blocks, key_only_first, off=0):
        return [(blocks[u], [off + u - 1, off + u] if u > 0 else [off + u])
                for u in range(1 if key_only_first else 0, len(blocks))]

    assert DILATED_PATTERNS == ((qb, 1), (4 * qb, 4), (16 * qb, 16)) and n16 == qb
    g = DIL_GROUP
    stats4 = (acc4_ref, m4_ref, l4_ref)
    stats = (acc_ref, m_ref, l_ref)

    def store_to(refs, starts):
        def store(i, vals):
            for ref, val in zip(refs, vals):
                put(ref, starts[i], 1, val)
        return store

    per16 = g // 4

    def p16_body(jj, carry):
        rj = [(r4, jj * per16 + dj) for dj in range(per16) for r4 in range(4)]
        blocks = [(r4 + 4 * j) * n16 for r4, j in rj]

        def store(i, vals):
            r4, j = rj[i]
            for ref, val in zip(stats4, vals):
                put(ref, r4 * n4 + j, 4, val)
        run_group((q16_ref, k16_ref, v16_ref), blocks, [(blocks[i], [i]) for i in range(len(rj))], None, store)
        return carry
    lax.fori_loop(0, 4 // per16, p16_body, 0)

    nb4 = n4 // qb
    per4 = g // nb4

    def p4_body(rr, carry):
        blocks, probs = [], []
        for dr in range(per4):
            base = pl.multiple_of((rr * per4 + dr) * n4, n4)
            chain = [base + u * qb for u in range(nb4)]
            probs += chain_probs(chain, False, len(blocks))
            blocks += chain
        run_group((q4_ref, k4_ref, v4_ref), blocks, probs, stats4, store_to(stats4, blocks))
        return carry
    lax.fori_loop(0, 4 // per4, p4_body, 0)

    def to_nat(c, carry):
        for r4 in range(4):
            for s4, s1 in zip(stats4, stats):
                put(s1, c * (4 * qb) + r4, 4, rows(s4, r4 * n4 + c * qb))
        return carry
    lax.fori_loop(0, n4 // qb, to_nat, 0)

    nb = seq // qb
    assert nb % g == 0
    nat = (q_ref, k_ref, v_ref)
    blocks0 = [u * qb for u in range(g)]
    run_group(nat, blocks0, chain_probs(blocks0, False), stats, store_to(stats, blocks0))

    def p1_body(i, carry):
        base = pl.multiple_of(i * (g * qb), g * qb)
        blocks = [base + (u - 1) * qb for u in range(g + 1)]
        run_group(nat, blocks, chain_probs(blocks, True), stats, store_to(stats, blocks[1:]))
        return carry
    lax.fori_loop(1, nb // g, p1_body, 0)

    out_ref[...] = (acc_ref[...] / l_ref[...]).astype(BF16)


def _dilated(qb, kb, vb, seq):
    b = qb.shape[0]
    npair = N_HEADS_B * HEAD_DIM // LANES
    spec = pl.BlockSpec((None, seq, LANES), lambda i, j: (i, 0, j))
    return pl.pallas_call(
        functools.partial(_dilated_kernel, seq=seq),
        grid=(b, npair),
        in_specs=[spec, spec, spec],
        out_specs=spec,
        out_shape=jax.ShapeDtypeStruct((b, seq, N_HEADS_B * HEAD_DIM), BF16),
        scratch_shapes=[pltpu.VMEM((seq, LANES), F32)] * 12 + [pltpu.VMEM((Q_BLOCK, 2 * Q_BLOCK), F32),
                                                              pltpu.VMEM((Q_BLOCK, Q_BLOCK), F32)],
        compiler_params=pltpu.CompilerParams(dimension_semantics=("arbitrary", "arbitrary"),
                                             vmem_limit_bytes=VMEM_LIMIT),
        name="dilated",
    )(qb, kb, vb)


def _memkv_kernel(mem_ref, g_ref, w_ref, k_ref, v_ref):
    m = _rms(mem_ref[...], g_ref[...]).astype(BF16)
    kv = _dot(m, w_ref[...])
    k_ref[...] = kv[:, :D_MODEL].astype(BF16)
    v_ref[...] = kv[:, D_MODEL:].astype(BF16)


def _memkv(mem2, g, w_kv):
    n = mem2.shape[0]
    tm = MEM_TOKENS
    row = lambda i: (i, 0)
    const = lambda i: (0, 0)
    return pl.pallas_call(
        _memkv_kernel,
        grid=(n // tm,),
        in_specs=[pl.BlockSpec((tm, D_MODEL), row), pl.BlockSpec((1, D_MODEL), const),
                  pl.BlockSpec((D_MODEL, 2 * D_MODEL), const)],
        out_specs=[pl.BlockSpec((tm, D_MODEL), row)] * 2,
        out_shape=[jax.ShapeDtypeStruct((n, D_MODEL), BF16)] * 2,
        compiler_params=pltpu.CompilerParams(dimension_semantics=("arbitrary",), vmem_limit_bytes=VMEM_LIMIT),
        name="memkv",
    )(mem2, g, w_kv)


def _cross_kernel(x_ref, oa_ref, ob_ref, wout_ref, g_ref, wq_ref, kc_ref, vc_ref, wo_ref, out_ref):
    half = N_HEADS_A * V_DIM
    x1 = x_ref[...] + _dot(oa_ref[...], wout_ref[:half, :]) + _dot(ob_ref[...], wout_ref[half:, :])
    qc = _dot(_rms(x1, g_ref[...]).astype(BF16), wq_ref[...]).astype(BF16)
    scale = CROSS_HEAD_DIM ** -0.5
    ocs = []
    for h in range(CROSS_HEADS):
        sl = slice(h * CROSS_HEAD_DIM, (h + 1) * CROSS_HEAD_DIM)
        s = _dot_nt(qc[:, sl], kc_ref[:, sl]) * scale
        p = jnp.exp(s - jnp.max(s, axis=1, keepdims=True))
        p = p / jnp.sum(p, axis=1, keepdims=True)
        ocs.append(_dot(p.astype(BF16), vc_ref[:, sl]).astype(BF16))
    oc = jnp.concatenate(ocs, axis=1)
    out_ref[...] = x1 + _dot(oc, wo_ref[...])


def _cross(x2, oa, ob, w_out, g, w_q, kc, vc, w_o, seq, tm):
    n = x2.shape[0]
    per_seq = seq // tm
    row = lambda i: (i, 0)
    const = lambda i: (0, 0)
    memmap = lambda i: (i // per_seq, 0)
    half = N_HEADS_A * V_DIM
    return pl.pallas_call(
        _cross_kernel,
        grid=(n // tm,),
        in_specs=[pl.BlockSpec((tm, D_MODEL), row), pl.BlockSpec((tm, half), row), pl.BlockSpec((tm, half), row),
                  pl.BlockSpec((D_MODEL, D_MODEL), const), pl.BlockSpec((1, D_MODEL), const),
                  pl.BlockSpec((D_MODEL, D_MODEL), const),
                  pl.BlockSpec((MEM_TOKENS, D_MODEL), memmap), pl.BlockSpec((MEM_TOKENS, D_MODEL), memmap),
                  pl.BlockSpec((D_MODEL, D_MODEL), const)],
        out_specs=pl.BlockSpec((tm, D_MODEL), row),
        out_shape=jax.ShapeDtypeStruct((n, D_MODEL), F32),
        compiler_params=pltpu.CompilerParams(dimension_semantics=("arbitrary",), vmem_limit_bytes=VMEM_LIMIT),
        name="cross",
    )(x2, oa, ob, w_out, g, w_q, kc, vc, w_o)


MLP_FF_CHUNK = 1024


def _mlp_kernel(x_ref, g_ref, wup_ref, wdown_ref, gf_ref, out_ref):
    x = x_ref[...]
    hm = _rms(x, g_ref[...]).astype(BF16)
    y = x
    for c in range(D_FF // MLP_FF_CHUNK):
        sl = slice(c * MLP_FF_CHUNK, (c + 1) * MLP_FF_CHUNK)
        u = jnp.maximum(_dot(hm, wup_ref[:, sl]), 0.0)
        y = y + _dot((u * u).astype(BF16), wdown_ref[sl, :])
    out_ref[...] = _rms(y, gf_ref[...])


def _mlp(x2, g, w_up, w_down, gf, tm):
    n = x2.shape[0]
    row = lambda i: (i, 0)
    const = lambda i: (0, 0)
    return pl.pallas_call(
        _mlp_kernel,
        grid=(n // tm,),
        in_specs=[pl.BlockSpec((tm, D_MODEL), row), pl.BlockSpec((1, D_MODEL), const),
                  pl.BlockSpec((D_MODEL, D_FF), const), pl.BlockSpec((D_FF, D_MODEL), const),
                  pl.BlockSpec((1, D_MODEL), const)],
        out_specs=pl.BlockSpec((tm, D_MODEL), row),
        out_shape=jax.ShapeDtypeStruct((n, D_MODEL), F32),
        compiler_params=pltpu.CompilerParams(dimension_semantics=("arbitrary",), vmem_limit_bytes=VMEM_LIMIT),
        name="mlp",
    )(x2, g, w_up, w_down, gf)


def _block_diag(w):
    h, a, b = w.shape
    eye = jnp.eye(h, dtype=w.dtype)
    return (eye[:, None, :, None] * w[:, :, None, :]).reshape(h * a, h * b)


def kernel(x, mem, norm_mix_g, w_in, kv_norm_g, w_uk, w_uv, w_out, norm_cross_g, norm_mem_g,
           w_q_cross, w_kv_cross, w_o_cross, norm_mlp_g, w_up, w_down, norm_final_g):
    b, seq, _ = x.shape
    assert seq == 2048 and w_in.shape[0] == 1, "kernel is specialised to SEQ=2048, DEPTH=1"
    tm = 512

    wi = w_in[0]
    col = lambda k: wi[:, _OFF[k]:_OFF[k + 1]]
    misc = jnp.concatenate([col(5), col(3), col(6), jnp.zeros((D_MODEL, LANES - 104), F32)], axis=1)
    w_cat = jnp.concatenate([col(0), col(1), col(2), misc, col(4), col(7)], axis=1).astype(BF16)
    wuk_bd = _block_diag(w_uk[0]).astype(BF16)
    wuv_bd = _block_diag(w_uv[0]).astype(BF16)
    tabs = _rope_tables(seq)

    x2 = x.reshape(b * seq, D_MODEL)
    qlat, qrope, kcat, ckv1, miscp, qidx, qb, kb, vb, kcatT = _inproj(
        x2, norm_mix_g[0][None], w_cat, wuk_bd, kv_norm_g[0][None], tabs, seq, tm)

    weffT = jnp.swapaxes(miscp.reshape(b, seq, LANES)[:, :, MISC_WI:MISC_WI + IDX_HEADS], 1, 2)
    o_a = _dsa(qlat.reshape(b, seq, -1), qrope.reshape(b, seq, -1), qidx.reshape(b, seq, -1), weffT,
               kcat.reshape(b, seq, -1), kcatT.reshape(b, 2 * KV_RANK, seq), ckv1.reshape(b, seq, -1), wuv_bd, seq)
    o_b = _dilated(qb.reshape(b, seq, -1), kb.reshape(b, seq, -1), vb.reshape(b, seq, -1), seq)

    kc, vc = _memkv(mem.reshape(b * MEM_TOKENS, D_MODEL), norm_mem_g[0][None], w_kv_cross[0].astype(BF16))
    xc = _cross(x2, o_a.reshape(b * seq, -1), o_b.reshape(b * seq, -1), w_out[0].astype(BF16),
                norm_cross_g[0][None], w_q_cross[0].astype(BF16), kc, vc, w_o_cross[0].astype(BF16), seq, tm)
    out = _mlp(xc, norm_mlp_g[0][None], w_up[0].astype(BF16), w_down[0].astype(BF16), norm_final_g[None], tm)
    return out.reshape(b, seq, D_MODEL)
```

```python
import functools

import numpy as np
import jax
import jax.numpy as jnp
from jax import lax
from jax.experimental import pallas as pl
from jax.experimental.pallas import tpu as pltpu

F32 = jnp.float32
BF16 = jnp.bfloat16
I32 = jnp.int32

D_MODEL = 1024
HEAD_DIM = 64
N_HEADS_A = 8
N_HEADS_B = 8
D_NOPE = 64
D_ROPE = 32
KV_RANK = 128
V_DIM = 64
IDX_HEADS = 8
IDX_DIM = 64
TOPK_MAX = 256
Q_BLOCK = 128
CROSS_HEADS = 4
CROSS_HEAD_DIM = 256
MEM_TOKENS = 256
D_FF = 4096
ROPE_THETA = 10000.0
NORM_EPS = 1e-6

LANES = 128
VMEM_LIMIT = 48 * 1024 * 1024
NEG = -1e30
INT_MIN = -2 ** 31
KEY_NEG_INF = -2139095041
DSA_Q_SCALE = float((D_NOPE + D_ROPE) ** -0.5 * np.log2(np.e))

_OFF = np.cumsum([0, 512, 256, 128, 32, 512, 64, 8, 1536])
C_QN, C_QR, C_CKV, C_MISC, C_QI, C_QB, C_KB, C_VB, C_END = 0, 512, 768, 896, 1024, 1536, 2048, 2560, 3072
MISC_KI, MISC_KR, MISC_WI = 0, 64, 96


def _dot(a, b):
    return jnp.dot(a, b, preferred_element_type=F32)


def _dot_nt(a, b):
    return lax.dot_general(a, b, (((1,), (1,)), ((), ())), preferred_element_type=F32)


def _rms(x, g):
    return x * lax.rsqrt(jnp.mean(x * x, axis=-1, keepdims=True) + NORM_EPS) * g


def _rope_tables(seq):
    pos = jnp.arange(seq, dtype=F32)[:, None]
    lane = np.arange(LANES)

    def tables(d, lanes_local, active):
        half = d // 2
        inv = ROPE_THETA ** (-jnp.arange(0, d, 2, dtype=F32) / d)
        ang = pos * inv[None, :]
        cos, sin = jnp.cos(ang), jnp.sin(ang)
        f = (lanes_local % d) % half
        first = jnp.asarray(((lanes_local % d) < half) & active)[None, :]
        second = jnp.asarray(((lanes_local % d) >= half) & active)[None, :]
        act = jnp.asarray(active)[None, :]
        c = jnp.where(act, cos[:, f], 0.0)
        sa = jnp.where(first, -sin[:, f], 0.0)
        sb = jnp.where(second, sin[:, f], 0.0)
        return c, sa, sb

    all_on = np.ones(LANES, bool)
    c64, sa64, sb64 = tables(64, lane, all_on)
    c32, sa32, sb32 = tables(32, lane, all_on)
    ki_on = lane < MISC_KR
    kr_on = (lane >= MISC_KR) & (lane < MISC_WI)
    ckr, sa16m, sb16m = tables(32, lane - MISC_KR, kr_on)
    cki, sa32m, sb32m = tables(64, lane, ki_on)
    w_scale = (IDX_HEADS ** -0.5) * (IDX_DIM ** -0.5)
    wi_on = jnp.asarray((lane >= MISC_WI) & (lane < MISC_WI + IDX_HEADS))[None, :]
    cosm = ckr + cki + jnp.where(wi_on, w_scale, 0.0)
    return jnp.stack([c64, sa64, sb64, c32, sa32, sb32, cosm, sa16m, sb16m, sa32m, sb32m], axis=0)


def _rope_lanes(x, cos, sa, sb, half):
    outs = []
    for c in range(x.shape[1] // LANES):
        xs = x[:, c * LANES:(c + 1) * LANES]
        outs.append(xs * cos + pltpu.roll(xs, LANES - half, 1) * sa + pltpu.roll(xs, half, 1) * sb)
    return outs[0] if len(outs) == 1 else jnp.concatenate(outs, axis=1)


def _inproj_kernel(x_ref, g_ref, w_ref, wuk_ref, kvg_ref, tab_ref,
                   qlat_ref, qrope_ref, kcat_ref, ckv1_ref, misc_ref, qidx_ref, qb_ref, kb_ref, vb_ref, kcatT_ref):
    h = _rms(x_ref[...], g_ref[...]).astype(BF16)

    def proj(c0, c1):
        return _dot(h, w_ref[:, c0:c1])

    c64, sa64, sb64 = tab_ref[0], tab_ref[1], tab_ref[2]
    c32, sa32, sb32 = tab_ref[3], tab_ref[4], tab_ref[5]

    qn = proj(C_QN, C_QR).astype(BF16)
    qlat_ref[...] = (_dot(qn, wuk_ref[...]) * DSA_Q_SCALE).astype(BF16)
    qrope_ref[...] = (_rope_lanes(proj(C_QR, C_CKV), c32, sa32, sb32, D_ROPE // 2) * DSA_Q_SCALE).astype(BF16)
    ckv32 = _rms(proj(C_CKV, C_MISC), kvg_ref[...])
    ckv = ckv32.astype(BF16)
    pm = proj(C_MISC, C_QI)
    misc = (pm * tab_ref[6]
            + pltpu.roll(pm, LANES - 16, 1) * tab_ref[7] + pltpu.roll(pm, 16, 1) * tab_ref[8]
            + pltpu.roll(pm, LANES - 32, 1) * tab_ref[9] + pltpu.roll(pm, 32, 1) * tab_ref[10])
    misc_ref[...] = misc
    kcat_ref[...] = jnp.concatenate([ckv, misc.astype(BF16)], axis=1)
    kcatT_ref[...] = jnp.concatenate([ckv32.T, misc.T], axis=0).astype(BF16)
    ckv1_ref[...] = jnp.concatenate([ckv, jnp.ones(ckv.shape, BF16)], axis=1)
    qidx_ref[...] = _rope_lanes(proj(C_QI, C_QB), c64, sa64, sb64, IDX_DIM // 2).astype(BF16)
    qb_ref[...] = _rope_lanes(proj(C_QB, C_KB), c64, sa64, sb64, HEAD_DIM // 2)
    kb_ref[...] = _rope_lanes(proj(C_KB, C_VB), c64, sa64, sb64, HEAD_DIM // 2)
    vb_ref[...] = proj(C_VB, C_END)


def _inproj(x2, g, w_cat, wuk_bd, kvg, tabs, seq, tm):
    n = x2.shape[0]
    per_seq = seq // tm
    row = lambda i: (i, 0)
    const = lambda i: (0, 0)
    outs = [(D_MODEL, BF16), (N_HEADS_A * D_ROPE, BF16), (2 * KV_RANK, BF16), (2 * KV_RANK, BF16), (LANES, F32),
            (IDX_HEADS * IDX_DIM, BF16), (512, F32), (512, F32), (512, F32)]
    return pl.pallas_call(
        _inproj_kernel,
        grid=(n // tm,),
        in_specs=[pl.BlockSpec((tm, D_MODEL), row),
                  pl.BlockSpec((1, D_MODEL), const),
                  pl.BlockSpec((D_MODEL, C_END), const),
                  pl.BlockSpec((N_HEADS_A * D_NOPE, N_HEADS_A * KV_RANK), const),
                  pl.BlockSpec((1, KV_RANK), const),
                  pl.BlockSpec((11, tm, LANES), lambda i: (0, i % per_seq, 0))],
        out_specs=[pl.BlockSpec((tm, w), row) for w, _ in outs]
        + [pl.BlockSpec((2 * KV_RANK, tm), lambda i: (i // per_seq, i % per_seq))],
        out_shape=[jax.ShapeDtypeStruct((n, w), dt) for w, dt in outs]
        + [jax.ShapeDtypeStruct((n // seq * 2 * KV_RANK, seq), BF16)],
        compiler_params=pltpu.CompilerParams(dimension_semantics=("arbitrary",), vmem_limit_bytes=VMEM_LIMIT),
        name="inproj",
    )(x2, g, w_cat, wuk_bd, kvg, tabs)


DSA_CK = 512
DSA_ATT_CK = 512
DSA_SELECT_SIZES = (512, 1024, 1536, 2048)
DSA_HEAD_GROUP = 4


def _dsa_index(qidx_ref, misc_ref, kcat_ref, ikey_ref, q0, nkc):
    ck = DSA_CK
    weff = misc_ref[...].T[MISC_WI:MISC_WI + IDX_HEADS, :]

    qi = qidx_ref[...]
    qi_all = jnp.concatenate([qi[:, h * IDX_DIM:(h + 1) * IDX_DIM] for h in range(IDX_HEADS)], axis=0)
    ki0 = MISC_KI
    t_pos = q0 + lax.broadcasted_iota(I32, (ck, Q_BLOCK), 1)

    def index_chunk(c):
        k0 = pl.multiple_of(c * ck, ck)
        kc = kcat_ref[pl.ds(k0, ck), ki0:ki0 + IDX_DIM]
        idx = jnp.zeros((ck, Q_BLOCK), F32)
        for g in range(IDX_HEADS // 2):
            lg = _dot_nt(kc, qi_all[2 * g * Q_BLOCK:(2 * g + 2) * Q_BLOCK, :])
            for hh in range(2):
                h = 2 * g + hh
                idx = idx + jnp.maximum(lg[:, hh * Q_BLOCK:(hh + 1) * Q_BLOCK], 0.0) * weff[h:h + 1, :]
        s_pos = k0 + lax.broadcasted_iota(I32, (ck, Q_BLOCK), 0)
        idx = jnp.where(idx == 0.0, 0.0, idx)
        idx = jnp.where(s_pos <= t_pos, idx, -jnp.inf)
        bits = pltpu.bitcast(idx, I32)
        ikey_ref[pl.ds(k0, ck), :] = bits ^ ((bits >> 31) & 0x7FFFFFFF)

    def index_pair(i, carry):
        index_chunk(2 * i)
        index_chunk(2 * i + 1)
        return carry
    lax.fori_loop(0, nkc // 2, index_pair, 0)

    @pl.when(nkc % 2 == 1)
    def _():
        index_chunk(nkc - 1)


def _radix_select(ikey_ref, planes_ref, sel_ref, nkc, grp, topk):
    ck = DSA_CK
    assert grp % 8 == 0 and (32 * grp) % ck == 0

    def fill_chunk(c, carry):
        ikey_ref[pl.ds(pl.multiple_of(c * ck, ck), ck), :] = jnp.full((ck, Q_BLOCK), KEY_NEG_INF, I32)
        return carry
    lax.fori_loop(nkc, 32 * grp // ck, fill_chunk, 0)

    def bit_transpose(v, carry):
        off = pl.multiple_of(v * 8, 8)
        a = [ikey_ref[pl.ds(grp * j + off, 8), :] for j in range(32)]
        j, msk = 16, 0x0000FFFF
        while j:
            k = 0
            while k < 32:
                t = (a[k] ^ lax.shift_right_logical(a[k + j], np.int32(j))) & np.int32(msk)
                a[k] = a[k] ^ t
                a[k + j] = a[k + j] ^ (t << np.int32(j))
                k = (k + j + 1) & ~j
            j >>= 1
            msk = (msk ^ (msk << j)) & 0xFFFFFFFF if j else msk
        a[0] = ~a[0]
        for i in range(32):
            planes_ref[pl.ds(grp * i + off, 8), :] = a[i]
        return carry
    lax.fori_loop(0, grp // 8, bit_transpose, 0)

    def count_bits(x):
        return jnp.sum(lax.population_count(x), axis=0, keepdims=True)

    def as_i32(u):
        return np.int32(u - (1 << 32) if u >= (1 << 31) else u)

    alive = jnp.full((grp, Q_BLOCK), -1, I32)
    n_gt = jnp.zeros((1, Q_BLOCK), I32)
    thr_u = jnp.zeros((1, Q_BLOCK), I32)
    for i in range(0, 32, 2):
        p_hi = planes_ref[grp * i:grp * (i + 1), :]
        p_lo = planes_ref[grp * (i + 1):grp * (i + 2), :]
        x1 = alive & p_hi
        x0 = alive ^ x1
        x11 = x1 & p_lo
        x10 = x1 ^ x11
        x01 = x0 & p_lo
        x00 = x0 ^ x01
        g3 = n_gt + count_bits(x11)
        g2 = g3 + count_bits(x10)
        g1 = g2 + count_bits(x01)
        t3, t2, t1 = g3 >= topk, g2 >= topk, g1 >= topk
        alive = jnp.where(t3, x11, jnp.where(t2, x10, jnp.where(t1, x01, x00)))
        n_gt = jnp.where(t3, n_gt, jnp.where(t2, g3, jnp.where(t1, g2, g1)))
        sh = 30 - i
        thr_u = thr_u | jnp.where(t3, as_i32(3 << sh), jnp.where(t2, as_i32(2 << sh),
                                                                   jnp.where(t1, as_i32(1 << sh), np.int32(0))))
    sel_ref[0:1, :] = thr_u ^ np.int32(INT_MIN)
    sel_ref[1:2, :] = topk - n_gt
    sel_ref[2:3, :] = n_gt + count_bits(alive)


def _dsa_bias(ikey_ref, bias_ref, xp_ref, sel_ref, q0, nkc, seq, topk):
    ck = DSA_CK
    t_pos = q0 + lax.broadcasted_iota(I32, (ck, Q_BLOCK), 1)
    thr, need, n_ge = sel_ref[0:1, :], sel_ref[1:2, :], sel_ref[2:3, :]

    def count(pred):
        def body(c, cnt):
            k0 = pl.multiple_of(c * ck, ck)
            v = ikey_ref[pl.ds(k0, ck), :]
            s_pos = k0 + lax.broadcasted_iota(I32, (ck, Q_BLOCK), 0)
            ind = pred(v, s_pos)
            return cnt + jnp.sum(ind.reshape(ck // 8, 8, Q_BLOCK), axis=0)
        cnt8 = lax.fori_loop(0, nkc, body, jnp.zeros((8, Q_BLOCK), I32))
        return jnp.sum(cnt8, axis=0, keepdims=True)

    xp_ref[...] = jnp.full((8, Q_BLOCK), seq, I32)

    @pl.when(jnp.max(n_ge) > topk)
    def _():
        x = jnp.zeros((1, Q_BLOCK), I32)
        for bit in range(int(np.log2(seq)) - 1, -1, -1):
            cand = x + np.int32(1 << bit)
            hc = count(lambda v, p, cand=cand: jnp.where(v == thr, jnp.where(p < cand, 1, 0), 0))
            x = jnp.where(hc < need, cand, x)
        xp_ref[...] = jnp.broadcast_to(x + 1, (8, Q_BLOCK))

    xp = xp_ref[0:1, :]

    def bias_chunk(c):
        k0 = pl.multiple_of(c * ck, ck)
        v = ikey_ref[pl.ds(k0, ck), :]
        s_pos = k0 + lax.broadcasted_iota(I32, (ck, Q_BLOCK), 0)
        tie = jnp.where(s_pos < xp, 0.0, NEG)
        b = jnp.where(v > thr, 0.0, jnp.where(v == thr, tie, NEG))
        b = jnp.where(s_pos <= t_pos, b, NEG).astype(F32)
        bias_ref[:, pl.ds(k0, ck)] = b.T

    def bias_pair(i, carry):
        bias_chunk(2 * i)
        bias_chunk(2 * i + 1)
        return carry
    lax.fori_loop(0, nkc // 2, bias_pair, 0)

    @pl.when(nkc % 2 == 1)
    def _():
        bias_chunk(nkc - 1)

    assert DSA_ATT_CK == ck


def _dsa_kernel(qlat_ref, qrope_ref, qidx_ref, misc_ref, kcat_ref, kcatT_ref, ckv_ref, wuv_ref,
                out_ref, ikey_ref, planes_ref, bias_ref, qall_ref, xp_ref, sel_ref, m_ref, acc_ref, pbuf_ref,
                *, seq, topk):
    blk = pl.program_id(1)
    q0 = blk * Q_BLOCK
    nkc = (q0 + Q_BLOCK + DSA_CK - 1) // DSA_CK
    few_keys = q0 + Q_BLOCK <= topk

    @pl.when(jnp.logical_not(few_keys))
    def _():
        _dsa_index(qidx_ref, misc_ref, kcat_ref, ikey_ref, q0, nkc)
        kmax = q0 + Q_BLOCK
        sizes = [n for n in DSA_SELECT_SIZES if n <= seq]
        for lo, n in zip([0] + sizes[:-1], sizes):
            @pl.when(jnp.logical_and(kmax > lo, kmax <= n))
            def _(n=n):
                _radix_select(ikey_ref, planes_ref, sel_ref, nkc, n // 32, topk)
        _dsa_bias(ikey_ref, bias_ref, xp_ref, sel_ref, q0, nkc, seq, topk)

    @pl.when(few_keys)
    def _():
        assert topk <= DSA_ATT_CK
        t_pos = q0 + lax.broadcasted_iota(I32, (Q_BLOCK, DSA_ATT_CK), 0)
        s_pos = lax.broadcasted_iota(I32, (Q_BLOCK, DSA_ATT_CK), 1)
        bias_ref[:, 0:DSA_ATT_CK] = jnp.where(s_pos <= t_pos, 0.0, NEG)

    zlo = jnp.zeros((Q_BLOCK, MISC_KR), BF16)
    zhi = jnp.zeros((Q_BLOCK, KV_RANK - MISC_KR - D_ROPE), BF16)
    for h in range(N_HEADS_A):
        qall_ref[h * Q_BLOCK:(h + 1) * Q_BLOCK, :] = jnp.concatenate(
            [qlat_ref[:, h * KV_RANK:(h + 1) * KV_RANK], zlo, qrope_ref[:, h * D_ROPE:(h + 1) * D_ROPE], zhi], axis=1)
    rows = N_HEADS_A * Q_BLOCK
    hg = DSA_HEAD_GROUP
    grows = hg * Q_BLOCK
    groups = [slice(g * grows, (g + 1) * grows) for g in range(N_HEADS_A // hg)]
    ck = DSA_ATT_CK
    n_att = (q0 + Q_BLOCK + ck - 1) // ck

    def scores(rs, k0):
        s = _dot(qall_ref[rs, :], kcatT_ref[:, pl.ds(k0, ck)])
        return (s.reshape(hg, Q_BLOCK, ck) + bias_ref[:, pl.ds(k0, ck)][None]).reshape(grows, ck)

    for rs in groups:
        s = scores(rs, 0)
        m0 = jnp.broadcast_to(jnp.max(s, axis=1, keepdims=True), (grows, LANES))
        pbuf_ref[0, rs, :] = jnp.exp2(s - pltpu.repeat(m0, ck // LANES, axis=1)).astype(BF16)
        m_ref[rs, :] = m0
    acc_ref[...] = jnp.zeros((rows, 2 * KV_RANK), F32)

    def attn_chunk(c):
        k0 = pl.multiple_of(c * ck, ck)
        cv_prev = ckv_ref[pl.ds(k0 - ck, ck), :]
        slot = c & 1
        olds = [(m_ref[rs, :], acc_ref[rs, :]) for rs in groups]
        news = []
        mm = [(scores(rs, k0), _dot(pbuf_ref[1 - slot, rs, :], cv_prev)) for rs in groups]
        for (s, pv_prev), (m_old, acc_old) in zip(mm, olds):
            m_new = jnp.maximum(m_old, jnp.max(s, axis=1, keepdims=True))
            alpha = jnp.exp2(m_old - m_new)
            p = jnp.exp2(s - pltpu.repeat(m_new, ck // LANES, axis=1)).astype(BF16)
            news.append((m_new, p, pltpu.repeat(alpha, 2, axis=1) * (acc_old + pv_prev)))
        for rs, (m_new, p, acc_new) in zip(groups, news):
            m_ref[rs, :] = m_new
            pbuf_ref[slot, rs, :] = p
            acc_ref[rs, :] = acc_new

    def attn_pair(i, carry):
        attn_chunk(2 * i + 1)
        attn_chunk(2 * i + 2)
        return carry
    lax.fori_loop(0, (n_att - 1) // 2, attn_pair, 0)

    @pl.when((n_att - 1) % 2 == 1)
    def _():
        attn_chunk(n_att - 1)

    last = n_att - 1
    cv_last = ckv_ref[pl.ds(pl.multiple_of(last * ck, ck), ck), :]
    for rs in groups:
        acc_ref[rs, :] = acc_ref[rs, :] + _dot(pbuf_ref[last & 1, rs, :], cv_last)

    o = (acc_ref[:, :KV_RANK] / acc_ref[:, KV_RANK:]).astype(BF16)
    o_lat = jnp.concatenate([o[h * Q_BLOCK:(h + 1) * Q_BLOCK, :] for h in range(N_HEADS_A)], axis=1)
    out_ref[...] = _dot(o_lat, wuv_ref[...]).astype(BF16)


def _dsa(qlat, qrope, qidx, misc, kcat, kcatT, ckv1, wuv_bd, seq):
    b = qlat.shape[0]
    nb = seq // Q_BLOCK
    topk = min(TOPK_MAX, seq // 4)
    blkmap = lambda i, j: (i, j, 0)
    seqmap = lambda i, j: (i, 0, 0)
    rows = N_HEADS_A * Q_BLOCK
    return pl.pallas_call(
        functools.partial(_dsa_kernel, seq=seq, topk=topk),
        grid=(b, nb),
        in_specs=[pl.BlockSpec((None, Q_BLOCK, N_HEADS_A * KV_RANK), blkmap),
                  pl.BlockSpec((None, Q_BLOCK, N_HEADS_A * D_ROPE), blkmap),
                  pl.BlockSpec((None, Q_BLOCK, IDX_HEADS * IDX_DIM), blkmap),
                  pl.BlockSpec((None, Q_BLOCK, LANES), blkmap),
                  pl.BlockSpec((None, seq, LANES), lambda i, j: (i, 0, KV_RANK // LANES)),
                  pl.BlockSpec((None, 2 * KV_RANK, seq), seqmap),
                  pl.BlockSpec((None, seq, 2 * KV_RANK), seqmap),
                  pl.BlockSpec((N_HEADS_A * KV_RANK, N_HEADS_A * V_DIM), lambda i, j: (0, 0))],
        out_specs=pl.BlockSpec((None, Q_BLOCK, N_HEADS_A * V_DIM), blkmap),
        out_shape=jax.ShapeDtypeStruct((b, seq, N_HEADS_A * V_DIM), BF16),
        scratch_shapes=[pltpu.VMEM((seq, Q_BLOCK), I32),
                        pltpu.VMEM((seq, Q_BLOCK), I32),
                        pltpu.VMEM((Q_BLOCK, seq), F32),
                        pltpu.VMEM((rows, 2 * KV_RANK), BF16),
                        pltpu.VMEM((8, Q_BLOCK), I32),
                        pltpu.VMEM((8, Q_BLOCK), I32),
                        pltpu.VMEM((rows, LANES), F32),
                        pltpu.VMEM((rows, 2 * KV_RANK), F32),
                        pltpu.VMEM((2, rows, DSA_ATT_CK), BF16)],
        compiler_params=pltpu.CompilerParams(dimension_semantics=("arbitrary", "arbitrary"),
                                             vmem_limit_bytes=VMEM_LIMIT),
        name="dsa",
    )(qlat, qrope, qidx, misc, kcat, kcatT, ckv1, wuv_bd)


DILATED_PATTERNS = ((128, 1), (512, 4), (2048, 16))
DIL_GROUP = 16


def _dilated_kernel(q_ref, k_ref, v_ref, out_ref, acc_ref, m_ref, l_ref, acc4_ref, m4_ref, l4_ref,
                    q4_ref, k4_ref, v4_ref, q16_ref, k16_ref, v16_ref, band_ref, tri_ref, *, seq):
    qb = Q_BLOCK
    lane = lax.broadcasted_iota(I32, (1, LANES), 1)
    head0 = lane < HEAD_DIM
    scale = HEAD_DIM ** -0.5

    a2 = lax.broadcasted_iota(I32, (qb, 2 * qb), 0)
    c2 = lax.broadcasted_iota(I32, (qb, 2 * qb), 1)
    band_ref[...] = jnp.where(c2 >= a2, jnp.where(c2 <= a2 + qb, 0.0, NEG), NEG)
    a1 = lax.broadcasted_iota(I32, (qb, qb), 0)
    c1 = lax.broadcasted_iota(I32, (qb, qb), 1)
    tri_ref[...] = jnp.where(c1 <= a1, 0.0, NEG)

    def rows(ref, start, stride=1):
        if stride == 1:
            return ref[pl.ds(start, qb), :]
        return ref[pl.ds(start, qb, stride=stride), :]

    def put(ref, start, stride, val):
        if stride == 1:
            ref[pl.ds(start, qb), :] = val
        else:
            ref[pl.ds(start, qb, stride=stride), :] = val

    n4, n16 = seq // 4, seq // 16
    for src, m4, m16 in ((q_ref, q4_ref, q16_ref), (k_ref, k4_ref, k16_ref), (v_ref, v4_ref, v16_ref)):
        def to_m4(c, carry, src=src, m4=m4):
            for r4 in range(4):
                m4[pl.ds(r4 * n4 + c * qb, qb), :] = rows(src, c * (4 * qb) + r4, 4)
            return carry
        lax.fori_loop(0, n4 // qb, to_m4, 0)

        def to_m16(r4, carry, m4=m4, m16=m16):
            for j in range(4):
                m16[pl.ds((r4 + 4 * j) * n16, qb), :] = rows(m4, r4 * n4 + j, 4)
            return carry
        lax.fori_loop(0, 4, to_m16, 0)

    def load_kv(kv_refs, start):
        kk = rows(kv_refs[0], start)
        v1 = jnp.concatenate([rows(kv_refs[1], start).astype(BF16), jnp.ones((qb, LANES), BF16)], axis=1)
        return jnp.where(head0, kk, 0.0).astype(BF16), jnp.where(head0, 0.0, kk).astype(BF16), v1

    def cat(xs):
        return xs[0] if len(xs) == 1 else jnp.concatenate(xs, axis=0)

    def attend_all(qs, kv_lists):
        scores = [[_dot_nt(q, cat([kv[hh] for kv in kvs])) for hh in range(2)] for q, kvs in zip(qs, kv_lists)]
        probs = []
        for ss, kvs in zip(scores, kv_lists):
            bias = tri_ref[...] if len(kvs) == 1 else band_ref[...]
            row = []
            for s in ss:
                s = s + bias
                m = jnp.broadcast_to(jnp.max(s, axis=1, keepdims=True), (qb, LANES))
                p = jnp.exp(s - pltpu.repeat(m, s.shape[1] // LANES, axis=1))
                row.append((p.astype(BF16), m))
            probs.append(row)
        outs = []
        for row, kvs in zip(probs, kv_lists):
            vv = cat([kv[2] for kv in kvs])
            pvs = [_dot(p, vv) for p, _ in row]
            outs.append((jnp.where(head0, pvs[0][:, :LANES], pvs[1][:, :LANES]),
                         jnp.where(head0, row[0][1], row[1][1]),
                         jnp.where(head0, pvs[0][:, LANES:], pvs[1][:, LANES:])))
        return outs

    def merged(old, new):
        mn = jnp.maximum(old[1], new[1])
        a_old = jnp.exp(old[1] - mn)
        a_new = jnp.exp(new[1] - mn)
        return old[0] * a_old + new[0] * a_new, mn, old[2] * a_old + new[2] * a_new

    def run_group(srcs, blocks, probs, stats_in, store):
        kvs = [load_kv(srcs[1:], s) for s in blocks]
        qs = [(rows(srcs[0], q_start) * scale).astype(BF16) for q_start, _ in probs]
        olds = [None if stats_in is None else tuple(rows(r, q_start) for r in stats_in) for q_start, _ in probs]
        news = attend_all(qs, [[kvs[b] for b in kb] for _, kb in probs])
        for i, (new, old) in enumerate(zip(news, olds)):
            store(i, new if old is None else merged(old, new))

    def chain_probs(blocks, key_only_first, off=0):
        return [(blocks[u], [off + u - 1, off + u] if u > 0 else [off + u])
                for u in range(1 if key_only_first else 0, len(blocks))]

    assert DILATED_PATTERNS == ((qb, 1), (4 * qb, 4), (16 * qb, 16)) and n16 == qb
    g = DIL_GROUP
    stats4 = (acc4_ref, m4_ref, l4_ref)
    stats = (acc_ref, m_ref, l_ref)

    def store_to(refs, starts):
        def store(i, vals):
            for ref, val in zip(refs, vals):
                put(ref, starts[i], 1, val)
        return store

    per16 = g // 4

    def p16_body(jj, carry):
        rj = [(r4, jj * per16 + dj) for dj in range(per16) for r4 in range(4)]
        blocks = [(r4 + 4 * j) * n16 for r4, j in rj]

        def store(i, vals):
            r4, j = rj[i]
            for ref, val in zip(stats4, vals):
                put(ref, r4 * n4 + j, 4, val)
        run_group((q16_ref, k16_ref, v16_ref), blocks, [(blocks[i], [i]) for i in range(len(rj))], None, store)
        return carry
    lax.fori_loop(0, 4 // per16, p16_body, 0)

    nb4 = n4 // qb
    per4 = g // nb4

    def p4_body(rr, carry):
        blocks, probs = [], []
        for dr in range(per4):
            base = pl.multiple_of((rr * per4 + dr) * n4, n4)
            chain = [base + u * qb for u in range(nb4)]
            probs += chain_probs(chain, False, len(blocks))
            blocks += chain
        run_group((q4_ref, k4_ref, v4_ref), blocks, probs, stats4, store_to(stats4, blocks))
        return carry
    lax.fori_loop(0, 4 // per4, p4_body, 0)

    def to_nat(c, carry):
        for r4 in range(4):
            for s4, s1 in zip(stats4, stats):
                put(s1, c * (4 * qb) + r4, 4, rows(s4, r4 * n4 + c * qb))
        return carry
    lax.fori_loop(0, n4 // qb, to_nat, 0)

    nb = seq // qb
    assert nb % g == 0
    nat = (q_ref, k_ref, v_ref)
    blocks0 = [u * qb for u in range(g)]
    run_group(nat, blocks0, chain_probs(blocks0, False), stats, store_to(stats, blocks0))

    def p1_body(i, carry):
        base = pl.multiple_of(i * (g * qb), g * qb)
        blocks = [base + (u - 1) * qb for u in range(g + 1)]
        run_group(nat, blocks, chain_probs(blocks, True), stats, store_to(stats, blocks[1:]))
        return carry
    lax.fori_loop(1, nb // g, p1_body, 0)

    out_ref[...] = (acc_ref[...] / l_ref[...]).astype(BF16)


def _dilated(qb, kb, vb, seq):
    b = qb.shape[0]
    npair = N_HEADS_B * HEAD_DIM // LANES
    spec = pl.BlockSpec((None, seq, LANES), lambda i, j: (i, 0, j))
    return pl.pallas_call(
        functools.partial(_dilated_kernel, seq=seq),
        grid=(b, npair),
        in_specs=[spec, spec, spec],
        out_specs=spec,
        out_shape=jax.ShapeDtypeStruct((b, seq, N_HEADS_B * HEAD_DIM), BF16),
        scratch_shapes=[pltpu.VMEM((seq, LANES), F32)] * 12 + [pltpu.VMEM((Q_BLOCK, 2 * Q_BLOCK), F32),
                                                              pltpu.VMEM((Q_BLOCK, Q_BLOCK), F32)],
        compiler_params=pltpu.CompilerParams(dimension_semantics=("arbitrary", "arbitrary"),
                                             vmem_limit_bytes=VMEM_LIMIT),
        name="dilated",
    )(qb, kb, vb)


def _memkv_kernel(mem_ref, g_ref, w_ref, k_ref, v_ref):
    m = _rms(mem_ref[...], g_ref[...]).astype(BF16)
    kv = _dot(m, w_ref[...])
    k_ref[...] = kv[:, :D_MODEL].astype(BF16)
    v_ref[...] = kv[:, D_MODEL:].astype(BF16)


def _memkv(mem2, g, w_kv):
    n = mem2.shape[0]
    tm = MEM_TOKENS
    row = lambda i: (i, 0)
    const = lambda i: (0, 0)
    return pl.pallas_call(
        _memkv_kernel,
        grid=(n // tm,),
        in_specs=[pl.BlockSpec((tm, D_MODEL), row), pl.BlockSpec((1, D_MODEL), const),
                  pl.BlockSpec((D_MODEL, 2 * D_MODEL), const)],
        out_specs=[pl.BlockSpec((tm, D_MODEL), row)] * 2,
        out_shape=[jax.ShapeDtypeStruct((n, D_MODEL), BF16)] * 2,
        compiler_params=pltpu.CompilerParams(dimension_semantics=("arbitrary",), vmem_limit_bytes=VMEM_LIMIT),
        name="memkv",
    )(mem2, g, w_kv)


CROSS_SUBTILES = 2


def _cross_kernel(x_ref, oa_ref, ob_ref, wout_ref, g_ref, wq_ref, kc_ref, vc_ref, wo_ref, out_ref):
    half = N_HEADS_A * V_DIM
    tm = x_ref.shape[0]
    subs = [slice(i * (tm // CROSS_SUBTILES), (i + 1) * (tm // CROSS_SUBTILES)) for i in range(CROSS_SUBTILES)]
    heads = [slice(h * CROSS_HEAD_DIM, (h + 1) * CROSS_HEAD_DIM) for h in range(CROSS_HEADS)]
    scale = CROSS_HEAD_DIM ** -0.5
    x1 = [x_ref[r, :] + _dot(oa_ref[r, :], wout_ref[:half, :]) + _dot(ob_ref[r, :], wout_ref[half:, :]) for r in subs]
    qc = [_dot(_rms(x, g_ref[...]).astype(BF16), wq_ref[...]).astype(BF16) for x in x1]
    scores = [[_dot_nt(q[:, sl], kc_ref[:, sl]) * scale for sl in heads] for q in qc]
    probs = []
    for ss in scores:
        row = []
        for s in ss:
            p = jnp.exp(s - jnp.max(s, axis=1, keepdims=True))
            row.append((p / jnp.sum(p, axis=1, keepdims=True)).astype(BF16))
        probs.append(row)
    ocs = [jnp.concatenate([_dot(p, vc_ref[:, sl]).astype(BF16) for p, sl in zip(row, heads)], axis=1) for row in probs]
    for r, x, oc in zip(subs, x1, ocs):
        out_ref[r, :] = x + _dot(oc, wo_ref[...])


def _cross(x2, oa, ob, w_out, g, w_q, kc, vc, w_o, seq, tm):
    n = x2.shape[0]
    per_seq = seq // tm
    row = lambda i: (i, 0)
    const = lambda i: (0, 0)
    memmap = lambda i: (i // per_seq, 0)
    half = N_HEADS_A * V_DIM
    return pl.pallas_call(
        _cross_kernel,
        grid=(n // tm,),
        in_specs=[pl.BlockSpec((tm, D_MODEL), row), pl.BlockSpec((tm, half), row), pl.BlockSpec((tm, half), row),
                  pl.BlockSpec((D_MODEL, D_MODEL), const), pl.BlockSpec((1, D_MODEL), const),
                  pl.BlockSpec((D_MODEL, D_MODEL), const),
                  pl.BlockSpec((MEM_TOKENS, D_MODEL), memmap), pl.BlockSpec((MEM_TOKENS, D_MODEL), memmap),
                  pl.BlockSpec((D_MODEL, D_MODEL), const)],
        out_specs=pl.BlockSpec((tm, D_MODEL), row),
        out_shape=jax.ShapeDtypeStruct((n, D_MODEL), F32),
        compiler_params=pltpu.CompilerParams(dimension_semantics=("arbitrary",), vmem_limit_bytes=VMEM_LIMIT),
        name="cross",
    )(x2, oa, ob, w_out, g, w_q, kc, vc, w_o)


MLP_FF_CHUNK = 1024


def _mlp_kernel(x_ref, g_ref, wup_ref, wdown_ref, gf_ref, out_ref):
    x = x_ref[...]
    hm = _rms(x, g_ref[...]).astype(BF16)
    y = x
    for c in range(D_FF // MLP_FF_CHUNK):
        sl = slice(c * MLP_FF_CHUNK, (c + 1) * MLP_FF_CHUNK)
        u = jnp.maximum(_dot(hm, wup_ref[:, sl]), 0.0)
        y = y + _dot((u * u).astype(BF16), wdown_ref[sl, :])
    out_ref[...] = _rms(y, gf_ref[...])


def _mlp(x2, g, w_up, w_down, gf, tm):
    n = x2.shape[0]
    row = lambda i: (i, 0)
    const = lambda i: (0, 0)
    return pl.pallas_call(
        _mlp_kernel,
        grid=(n // tm,),
        in_specs=[pl.BlockSpec((tm, D_MODEL), row), pl.BlockSpec((1, D_MODEL), const),
                  pl.BlockSpec((D_MODEL, D_FF), const), pl.BlockSpec((D_FF, D_MODEL), const),
                  pl.BlockSpec((1, D_MODEL), const)],
        out_specs=pl.BlockSpec((tm, D_MODEL), row),
        out_shape=jax.ShapeDtypeStruct((n, D_MODEL), F32),
        compiler_params=pltpu.CompilerParams(dimension_semantics=("arbitrary",), vmem_limit_bytes=VMEM_LIMIT),
        name="mlp",
    )(x2, g, w_up, w_down, gf)


def _block_diag(w):
    h, a, b = w.shape
    eye = jnp.eye(h, dtype=w.dtype)
    return (eye[:, None, :, None] * w[:, :, None, :]).reshape(h * a, h * b)


def kernel(x, mem, norm_mix_g, w_in, kv_norm_g, w_uk, w_uv, w_out, norm_cross_g, norm_mem_g,
           w_q_cross, w_kv_cross, w_o_cross, norm_mlp_g, w_up, w_down, norm_final_g):
    b, seq, _ = x.shape
    assert seq == 2048 and w_in.shape[0] == 1, "kernel is specialised to SEQ=2048, DEPTH=1"
    tm = 512

    wi = w_in[0]
    col = lambda k: wi[:, _OFF[k]:_OFF[k + 1]]
    misc = jnp.concatenate([col(5), col(3), col(6), jnp.zeros((D_MODEL, LANES - 104), F32)], axis=1)
    w_cat = jnp.concatenate([col(0), col(1), col(2), misc, col(4), col(7)], axis=1).astype(BF16)
    wuk_bd = _block_diag(w_uk[0]).astype(BF16)
    wuv_bd = _block_diag(w_uv[0]).astype(BF16)
    tabs = _rope_tables(seq)

    x2 = x.reshape(b * seq, D_MODEL)
    qlat, qrope, kcat, ckv1, miscp, qidx, qb, kb, vb, kcatT = _inproj(
        x2, norm_mix_g[0][None], w_cat, wuk_bd, kv_norm_g[0][None], tabs, seq, tm)

    o_a = _dsa(qlat.reshape(b, seq, -1), qrope.reshape(b, seq, -1), qidx.reshape(b, seq, -1),
               miscp.reshape(b, seq, -1),
               kcat.reshape(b, seq, -1), kcatT.reshape(b, 2 * KV_RANK, seq), ckv1.reshape(b, seq, -1), wuv_bd, seq)
    o_b = _dilated(qb.reshape(b, seq, -1), kb.reshape(b, seq, -1), vb.reshape(b, seq, -1), seq)

    kc, vc = _memkv(mem.reshape(b * MEM_TOKENS, D_MODEL), norm_mem_g[0][None], w_kv_cross[0].astype(BF16))
    xc = _cross(x2, o_a.reshape(b * seq, -1), o_b.reshape(b * seq, -1), w_out[0].astype(BF16),
                norm_cross_g[0][None], w_q_cross[0].astype(BF16), kc, vc, w_o_cross[0].astype(BF16), seq, tm)
    out = _mlp(xc, norm_mlp_g[0][None], w_up[0].astype(BF16), w_down[0].astype(BF16), norm_final_g[None], tm)
    return out.reshape(b, seq, D_MODEL)
```
